```python
import math
import jax, jax.numpy as jnp
from jax import lax
import numpy as np

D_MODEL = 4096
BATCH = 1
SEQ = 8192
DEPTH = 1

N_Q_HEADS = 32
N_KV_HEADS = 8
HEAD_DIM = 128
ATTN_WIDTH = N_Q_HEADS * HEAD_DIM
KV_WIDTH = N_KV_HEADS * HEAD_DIM
WINDOW = 128
BLOCK = WINDOW
ROPE_DIM = HEAD_DIM // 4
ROPE_THETA = 500000.0
SSM_WIDTH = D_MODEL // 2
SSM_GROUP = 16
SSM_GROUPS = SSM_WIDTH // SSM_GROUP
SSM_STATE = 64
DT_MIN = 0.001
DT_MAX = 0.1
D_FF = 4 * D_MODEL
N_BRANCHES = 2
IN_WIDTH = ATTN_WIDTH + 2 * KV_WIDTH + SSM_WIDTH + N_BRANCHES * D_MODEL
N_MOD = 6
EPS = 1e-6
MASK_VALUE = -1e30

kernel_name = "hybrid_swa_sinks_s5_gated_adaln_block"


def rmsnorm(x, g):
    xf = x.astype(jnp.float32)
    y = xf * lax.rsqrt(jnp.mean(xf * xf, axis=-1, keepdims=True) + EPS)
    return (y * g.astype(jnp.float32)).astype(x.dtype)


def partial_rope(x, positions):
    half = ROPE_DIM // 2
    inv_freq = jnp.power(jnp.float32(ROPE_THETA), -2.0 * jnp.arange(half, dtype=jnp.float32) / ROPE_DIM)
    ang = positions.astype(jnp.float32)[..., None] * inv_freq
    cos = jnp.cos(ang)[:, :, None, :]
    sin = jnp.sin(ang)[:, :, None, :]
    xf = x.astype(jnp.float32)
    x1 = xf[..., :half]
    x2 = xf[..., half:ROPE_DIM]
    rot = jnp.concatenate([x1 * cos - x2 * sin, x2 * cos + x1 * sin], axis=-1)
    return jnp.concatenate([rot.astype(x.dtype), x[..., ROPE_DIM:]], axis=-1)


def sliding_window_attention(q, k, v, sinks):
    b, s = q.shape[0], q.shape[1]
    nb = s // BLOCK
    rep = N_Q_HEADS // N_KV_HEADS
    qb = q.reshape(b, nb, BLOCK, N_KV_HEADS, rep, HEAD_DIM)

    def band(t):
        tb = t.reshape(b, nb, BLOCK, N_KV_HEADS, HEAD_DIM)
        prev = jnp.pad(tb[:, :-1], ((0, 0), (1, 0), (0, 0), (0, 0), (0, 0)))
        return jnp.concatenate([prev, tb], axis=2)

    kb = band(k)
    vb = band(v)
    scores = jnp.einsum('bnqgrd,bnkgd->bngrqk', qb, kb).astype(jnp.float32) * (HEAD_DIM ** -0.5)

    blk = jnp.arange(nb)[:, None]
    q_idx = blk * BLOCK + jnp.arange(BLOCK)[None, :]
    k_idx = (blk - 1) * BLOCK + jnp.arange(2 * BLOCK)[None, :]
    rel = q_idx[:, :, None] - k_idx[:, None, :]
    valid = (rel >= 0) & (rel < WINDOW) & (k_idx[:, None, :] >= 0)
    scores = jnp.where(valid[None, :, None, None], scores, MASK_VALUE)

    sink = sinks.astype(jnp.float32).reshape(1, 1, N_KV_HEADS, rep, 1, 1)
    m = jnp.maximum(jnp.max(scores, axis=-1, keepdims=True), sink)
    p = jnp.exp(scores - m)
    denom = jnp.sum(p, axis=-1, keepdims=True) + jnp.exp(sink - m)
    probs = (p / denom).astype(vb.dtype)
    out = jnp.einsum('bngrqk,bnkgd->bnqgrd', probs, vb)
    return out.reshape(b, s, ATTN_WIDTH)


def _complex_scan_op(e1, e2):
    a1r, a1i, b1r, b1i = e1
    a2r, a2i, b2r, b2i = e2
    ar = a2r * a1r - a2i * a1i
    ai = a2r * a1i + a2i * a1r
    br = a2r * b1r - a2i * b1i + b2r
    bi = a2r * b1i + a2i * b1r + b2i
    return (ar, ai, br, bi)


def s5_layer(u, lam_re, lam_im, log_dt, b_re, b_im, c_re, c_im, d_skip):
    bsz, s = u.shape[0], u.shape[1]
    uf = u.astype(jnp.float32).reshape(bsz, s, SSM_GROUPS, SSM_GROUP)
    lr = lam_re.astype(jnp.float32)
    li = lam_im.astype(jnp.float32)
    dt = jnp.exp(log_dt.astype(jnp.float32))[:, None]
    mag = jnp.exp(lr * dt)
    abar_re = mag * jnp.cos(li * dt)
    abar_im = mag * jnp.sin(li * dt)
    den = lr * lr + li * li
    nr = abar_re - 1.0
    coef_re = (nr * lr + abar_im * li) / den
    coef_im = (abar_im * lr - nr * li) / den
    br_ = b_re.astype(jnp.float32)
    bi_ = b_im.astype(jnp.float32)
    bbar_re = coef_re[:, :, None] * br_ - coef_im[:, :, None] * bi_
    bbar_im = coef_re[:, :, None] * bi_ + coef_im[:, :, None] * br_
    bu_re = jnp.einsum('bsgc,gpc->bsgp', uf, bbar_re)
    bu_im = jnp.einsum('bsgc,gpc->bsgp', uf, bbar_im)
    a_re = jnp.broadcast_to(abar_re, bu_re.shape)
    a_im = jnp.broadcast_to(abar_im, bu_im.shape)
    _, _, st_re, st_im = lax.associative_scan(_complex_scan_op, (a_re, a_im, bu_re, bu_im), axis=1)
    y = (jnp.einsum('bsgp,gcp->bsgc', st_re, c_re.astype(jnp.float32))
         - jnp.einsum('bsgp,gcp->bsgc', st_im, c_im.astype(jnp.float32))
         + d_skip.astype(jnp.float32).reshape(SSM_GROUPS, SSM_GROUP) * uf)
    return y.reshape(bsz, s, SSM_WIDTH).astype(u.dtype)


def setup_inputs(seed: int = 0) -> dict:
    key = jax.random.key(seed)
    ks = jax.random.split(key, 24)
    f32 = jnp.float32
    L = DEPTH
    x = jax.random.normal(ks[0], (BATCH, SEQ, D_MODEL), f32)
    c = jax.random.normal(ks[1], (BATCH, D_MODEL), f32)
    offset = jax.random.randint(ks[2], (BATCH, 1), 0, 1024, dtype=jnp.int32)
    positions = offset + jnp.arange(SEQ, dtype=jnp.int32)[None, :]
    w_ada = jax.random.normal(ks[3], (L, D_MODEL, N_MOD * D_MODEL), f32) * (0.5 * D_MODEL ** -0.5)
    b_ada = jax.random.normal(ks[4], (L, N_MOD * D_MODEL), f32) * 0.02
    norm1_g = 1.0 + 0.02 * jax.random.normal(ks[5], (L, D_MODEL), f32)
    norm2_g = 1.0 + 0.02 * jax.random.normal(ks[6], (L, D_MODEL), f32)
    w_in = jax.random.normal(ks[7], (L, D_MODEL, IN_WIDTH), f32) * D_MODEL ** -0.5
    q_norm_g = 1.0 + 0.02 * jax.random.normal(ks[8], (L, HEAD_DIM), f32)
    k_norm_g = 1.0 + 0.02 * jax.random.normal(ks[9], (L, HEAD_DIM), f32)
    attn_sinks = jax.random.normal(ks[10], (L, N_Q_HEADS), f32)
    n = jnp.arange(SSM_STATE, dtype=f32)
    ssm_lam_re = jnp.broadcast_to(jnp.full((SSM_STATE,), -0.5, f32), (L, SSM_GROUPS, SSM_STATE))
    ssm_lam_im = jnp.broadcast_to(jnp.float32(math.pi) * n, (L, SSM_GROUPS, SSM_STATE))
    ssm_log_dt = jax.random.uniform(ks[11], (L, SSM_GROUPS), f32, math.log(DT_MIN), math.log(DT_MAX))
    b_std = (2.0 * SSM_GROUP) ** -0.5
    ssm_b_re = jax.random.normal(ks[12], (L, SSM_GROUPS, SSM_STATE, SSM_GROUP), f32) * b_std
    ssm_b_im = jax.random.normal(ks[13], (L, SSM_GROUPS, SSM_STATE, SSM_GROUP), f32) * b_std
    c_std = (2.0 * SSM_STATE) ** -0.5
    ssm_c_re = jax.random.normal(ks[14], (L, SSM_GROUPS, SSM_GROUP, SSM_STATE), f32) * c_std
    ssm_c_im = jax.random.normal(ks[15], (L, SSM_GROUPS, SSM_GROUP, SSM_STATE), f32) * c_std
    ssm_d = jax.random.normal(ks[16], (L, SSM_WIDTH), f32)
    w_glu = jax.random.normal(ks[17], (L, SSM_WIDTH, 2 * D_MODEL), f32) * SSM_WIDTH ** -0.5
    w_out = jax.random.normal(ks[18], (L, D_MODEL, D_MODEL), f32) * D_MODEL ** -0.5
    w_ff1 = jax.random.normal(ks[19], (L, D_MODEL, D_FF), f32) * D_MODEL ** -0.5
    w_ff2 = jax.random.normal(ks[20], (L, D_FF, D_MODEL), f32) * D_FF ** -0.5
    return {"x": x, "c": c, "positions": positions, "w_ada": w_ada, "b_ada": b_ada,
            "norm1_g": norm1_g, "norm2_g": norm2_g, "w_in": w_in, "q_norm_g": q_norm_g,
            "k_norm_g": k_norm_g, "attn_sinks": attn_sinks, "ssm_lam_re": ssm_lam_re,
            "ssm_lam_im": ssm_lam_im, "ssm_log_dt": ssm_log_dt, "ssm_b_re": ssm_b_re,
            "ssm_b_im": ssm_b_im, "ssm_c_re": ssm_c_re, "ssm_c_im": ssm_c_im, "ssm_d": ssm_d,
            "w_glu": w_glu, "w_out": w_out, "w_ff1": w_ff1, "w_ff2": w_ff2}


def reference(x, c, positions, w_ada, b_ada, norm1_g, norm2_g, w_in, q_norm_g, k_norm_g,
              attn_sinks, ssm_lam_re, ssm_lam_im, ssm_log_dt, ssm_b_re, ssm_b_im, ssm_c_re,
              ssm_c_im, ssm_d, w_glu, w_out, w_ff1, w_ff2):
    bsz, s = x.shape[0], x.shape[1]
    c_act = jax.nn.silu(c)
    for l in range(DEPTH):
        mod = (c_act @ w_ada[l] + b_ada[l])[:, None, :]
        sh1, sc1, g1, sh2, sc2, g2 = jnp.split(mod, N_MOD, axis=-1)

        h = rmsnorm(x, norm1_g[l]) * (1.0 + sc1) + sh1
        proj = h @ w_in[l]
        o1 = ATTN_WIDTH
        o2 = o1 + KV_WIDTH
        o3 = o2 + KV_WIDTH
        o4 = o3 + SSM_WIDTH
        q = proj[..., :o1].reshape(bsz, s, N_Q_HEADS, HEAD_DIM)
        k = proj[..., o1:o2].reshape(bsz, s, N_KV_HEADS, HEAD_DIM)
        v = proj[..., o2:o3].reshape(bsz, s, N_KV_HEADS, HEAD_DIM)
        u = proj[..., o3:o4]
        gate_a = proj[..., o4:o4 + D_MODEL]
        gate_s = proj[..., o4 + D_MODEL:]

        q = partial_rope(rmsnorm(q, q_norm_g[l]), positions)
        k = partial_rope(rmsnorm(k, k_norm_g[l]), positions)
        attn = sliding_window_attention(q, k, v, attn_sinks[l])

        y_ssm = s5_layer(u, ssm_lam_re[l], ssm_lam_im[l], ssm_log_dt[l], ssm_b_re[l], ssm_b_im[l],
                         ssm_c_re[l], ssm_c_im[l], ssm_d[l])
        glu = jax.nn.gelu(y_ssm) @ w_glu[l]
        ssm_branch = glu[..., :D_MODEL] * jax.nn.sigmoid(glu[..., D_MODEL:])

        merged = jax.nn.sigmoid(gate_a) * attn + jax.nn.sigmoid(gate_s) * ssm_branch
        x = x + g1 * (merged @ w_out[l])

        h2 = rmsnorm(x, norm2_g[l]) * (1.0 + sc2) + sh2
        ff = jnp.square(jax.nn.relu(h2 @ w_ff1[l])) @ w_ff2[l]
        x = x + g2 * ff
    return x
```

```python
import functools
import math

import jax
import jax.numpy as jnp
from jax import lax
from jax.experimental import pallas as pl
from jax.experimental.pallas import tpu as pltpu

F32 = jnp.float32
BF16 = jnp.bfloat16

LANES = 128
SUBLANES = 8
MXU_DIM = 256
VMEM_LIMIT_BYTES = 56 * 1024 * 1024

N_Q_HEADS = 32
N_KV_HEADS = 8
HEAD_DIM = 128
REP = N_Q_HEADS // N_KV_HEADS
WINDOW = 128
ROPE_DIM = HEAD_DIM // 4
ROPE_HALF = ROPE_DIM // 2
ROPE_THETA = 500000.0
SSM_GROUP = 16
SSM_STATE = 64
N_MOD = 6
EPS = 1e-6
MASK_VALUE = -1e30
CHUNK = MXU_DIM // SSM_GROUP


def _cparams(*sem):
    return pltpu.CompilerParams(dimension_semantics=sem, vmem_limit_bytes=VMEM_LIMIT_BYTES)


def _mod_kernel(c_ref, w_ref, b_ref, o_ref, *, rk):
    d, tn = w_ref.shape

    def body(r, acc):
        rows = pl.ds(pl.multiple_of(r * rk, rk), rk)
        c = c_ref[rows, :]
        ca = c * jax.nn.sigmoid(c)
        prod = w_ref[rows, :] * ca
        return acc + jnp.sum(prod.reshape(rk // SUBLANES, SUBLANES, tn), axis=0)

    acc = lax.fori_loop(0, d // rk, body, jnp.zeros((SUBLANES, tn), F32))
    o_ref[...] = jnp.sum(acc, axis=0, keepdims=True) + b_ref[...]


def _modulation(c_col, w_ada, b_ada_row, *, tn=512, rk=512):
    d, n = w_ada.shape
    return pl.pallas_call(
        functools.partial(_mod_kernel, rk=rk),
        grid=(n // tn,),
        in_specs=[
            pl.BlockSpec((d, 1), lambda j: (0, 0)),
            pl.BlockSpec((d, tn), lambda j: (0, j)),
            pl.BlockSpec((1, tn), lambda j: (0, j)),
        ],
        out_specs=pl.BlockSpec((1, tn), lambda j: (0, j)),
        out_shape=jax.ShapeDtypeStruct((1, n), F32),
        compiler_params=_cparams("arbitrary"),
        name="mod",
    )(c_col, w_ada, b_ada_row)


def _norm_kernel(x_ref, g_ref, sh_ref, sc_ref, o_ref):
    x = x_ref[...]
    ms = jnp.mean(x * x, axis=-1, keepdims=True)
    y = x * lax.rsqrt(ms + EPS) * g_ref[...]
    o_ref[...] = (y * (1.0 + sc_ref[...]) + sh_ref[...]).astype(o_ref.dtype)


def _norm_mod(x, g_row, mod, shift_idx, *, tm=256):
    s, d = x.shape
    return pl.pallas_call(
        _norm_kernel,
        grid=(s // tm,),
        in_specs=[
            pl.BlockSpec((tm, d), lambda i: (i, 0)),
            pl.BlockSpec((1, d), lambda i: (0, 0)),
            pl.BlockSpec((1, d), lambda i: (0, shift_idx)),
            pl.BlockSpec((1, d), lambda i: (0, shift_idx + 1)),
        ],
        out_specs=pl.BlockSpec((tm, d), lambda i: (i, 0)),
        out_shape=jax.ShapeDtypeStruct((s, d), BF16),
        compiler_params=_cparams("arbitrary"),
        name="norm",
    )(x, g_row, mod, mod)


def _rope_kernel(pos_ref, c_ref, sa_ref, sb_ref):
    pos = pos_ref[...].astype(F32)
    lane = lax.broadcasted_iota(jnp.int32, (1, LANES), 1)
    idx = (lane & (ROPE_HALF - 1)).astype(F32)
    inv_freq = jnp.power(jnp.float32(ROPE_THETA), -2.0 * idx / ROPE_DIM)
    ang = pos * inv_freq
    cos = jnp.cos(ang)
    sin = jnp.sin(ang)
    c_ref[...] = jnp.where(lane < ROPE_DIM, cos, 1.0)
    sa_ref[...] = jnp.where(lane < ROPE_HALF, -sin, 0.0)
    sb_ref[...] = jnp.where((lane >= ROPE_HALF) & (lane < ROPE_DIM), sin, 0.0)


def _rope_tables(pos_col, *, tm=1024):
    s = pos_col.shape[0]
    tm = min(tm, s)
    spec = pl.BlockSpec((tm, LANES), lambda i: (i, 0))
    shp = jax.ShapeDtypeStruct((s, LANES), F32)
    return pl.pallas_call(
        _rope_kernel,
        grid=(s // tm,),
        in_specs=[pl.BlockSpec((tm, 1), lambda i: (i, 0))],
        out_specs=[spec, spec, spec],
        out_shape=[shp, shp, shp],
        compiler_params=_cparams("arbitrary"),
        name="rope",
    )(pos_col)


def _proj_kernel(h_ref, w_ref, qg_ref, kg_ref, rc_ref, rsa_ref, rsb_ref, o_ref, acc_ref, *, n_q, n_qk, n_qkv):
    j = pl.program_id(1)
    tn = o_ref.shape[1]
    acc_ref[...] = jnp.dot(h_ref[...], w_ref[...].astype(BF16), preferred_element_type=F32)

    @pl.when(j < n_qk)
    def _():
        g = jnp.where(j < n_q, qg_ref[...], kg_ref[...])
        c, sa, sb = rc_ref[...], rsa_ref[...], rsb_ref[...]
        for hh in range(tn // HEAD_DIM):
            cols = slice(hh * HEAD_DIM, (hh + 1) * HEAD_DIM)
            xh = acc_ref[:, cols]
            ms = jnp.mean(xh * xh, axis=-1, keepdims=True)
            y = xh * lax.rsqrt(ms + EPS) * g
            y = y * c + pltpu.roll(y, HEAD_DIM - ROPE_HALF, 1) * sa + pltpu.roll(y, ROPE_HALF, 1) * sb
            o_ref[:, cols] = y.astype(o_ref.dtype)

    @pl.when((j >= n_qk) & (j < n_qkv))
    def _():
        o_ref[...] = acc_ref[...].astype(o_ref.dtype)

    @pl.when(j >= n_qkv)
    def _():
        o_ref[...] = jax.nn.sigmoid(acc_ref[...]).astype(o_ref.dtype)


def _proj(h, w_in, qg_row, kg_row, rope_c, rope_sa, rope_sb, *, d_model, kv_width, ssm_width, tm=1024, tn=512):
    s, d = h.shape
    tm = min(tm, s)
    n_q = d_model // tn
    n_qk = n_q + kv_width // tn
    n_qkv = n_qk + kv_width // tn
    n_u = ssm_width // tn
    n_out = n_qkv + 2 * d_model // tn
    tab = pl.BlockSpec((tm, LANES), lambda i, j: (i, 0))
    row = pl.BlockSpec((1, HEAD_DIM), lambda i, j: (0, 0))
    return pl.pallas_call(
        functools.partial(_proj_kernel, n_q=n_q, n_qk=n_qk, n_qkv=n_qkv),
        grid=(s // tm, n_out),
        in_specs=[
            pl.BlockSpec((tm, d), lambda i, j: (i, 0)),
            pl.BlockSpec((d, tn), lambda i, j: (0, jnp.where(j < n_qkv, j, j + n_u))),
            row, row, tab, tab, tab,
        ],
        out_specs=pl.BlockSpec((tm, tn), lambda i, j: (i, j)),
        out_shape=jax.ShapeDtypeStruct((s, n_out * tn), BF16),
        scratch_shapes=[pltpu.VMEM((tm, tn), F32)],
        compiler_params=_cparams("arbitrary", "arbitrary"),
        name="proj",
    )(h, w_in, qg_row, kg_row, rope_c, rope_sa, rope_sb)


def _proju_kernel(h_ref, w_ref, o_ref, *, d):
    wb = w_ref[...].astype(BF16)
    for p in range(o_ref.shape[0]):
        acc = jnp.dot(h_ref[:, p * d:(p + 1) * d], wb, preferred_element_type=F32)
        o_ref[p] = acc.T


def _proj_u(h_phase, w_in, *, d, col0, ssm_width, phases=2, tn=512):
    nc = h_phase.shape[0]
    return pl.pallas_call(
        functools.partial(_proju_kernel, d=d),
        grid=(CHUNK // phases, ssm_width // tn),
        in_specs=[
            pl.BlockSpec((nc, phases * d), lambda t, j: (0, t)),
            pl.BlockSpec((d, tn), lambda t, j: (0, col0 // tn + j)),
        ],
        out_specs=pl.BlockSpec((phases, tn, nc), lambda t, j: (t, j, 0)),
        out_shape=jax.ShapeDtypeStruct((CHUNK, ssm_width, nc), F32),
        compiler_params=_cparams("arbitrary", "arbitrary"),
        name="proj_u",
    )(h_phase, w_in)


def _cmul(ar, ai, br, bi):
    return ar * br - ai * bi, ar * bi + ai * br


def _s5prep_kernel(ldt_ref, lrc_ref, lic_ref, lrr_ref, lir_ref, bre_ref, bim_ref, cre_ref, cim_ref,
                   cre2_ref, cim2_ref, m_ref, w_ref, v_ref, ap_ref, *, n_levels):
    p_ = SSM_STATE
    kdim = CHUNK * SSM_GROUP
    dt = jnp.exp(ldt_ref[0])

    def discretise(lr, li):
        mag = jnp.exp(lr * dt)
        return mag * jnp.cos(li * dt), mag * jnp.sin(li * dt)

    lr, li = lrc_ref[0], lic_ref[0]
    are, aim = discretise(lr, li)
    den = lr * lr + li * li
    nr = are - 1.0
    coef_re = (nr * lr + aim * li) / den
    coef_im = (aim * lr - nr * li) / den
    bt_re, bt_im = bre_ref[0], bim_ref[0]
    bbar_re = coef_re * bt_re - coef_im * bt_im
    bbar_im = coef_re * bt_im + coef_im * bt_re

    pow_re = [jnp.ones_like(are)]
    pow_im = [jnp.zeros_like(are)]
    for _ in range(CHUNK):
        nre, nim = _cmul(pow_re[-1], pow_im[-1], are, aim)
        pow_re.append(nre)
        pow_im.append(nim)

    blk = lax.broadcasted_iota(jnp.int32, (1, kdim), 1) // SSM_GROUP
    tab_re = jnp.zeros((p_, kdim), F32)
    tab_im = jnp.zeros((p_, kdim), F32)
    for m in range(CHUNK):
        sel = blk == m
        tab_re = jnp.where(sel, pow_re[CHUNK - 1 - m], tab_re)
        tab_im = jnp.where(sel, pow_im[CHUNK - 1 - m], tab_im)
    g_re, g_im = _cmul(tab_re, tab_im, bbar_re, bbar_im)
    w_ref[0] = jnp.concatenate([g_re, g_im], axis=0).astype(w_ref.dtype)

    hi = lax.Precision.HIGHEST
    k_rev = (jnp.dot(cre_ref[0], g_re, precision=hi, preferred_element_type=F32)
             - jnp.dot(cim_ref[0], g_im, precision=hi, preferred_element_type=F32))
    k_ext = jnp.concatenate([k_rev, jnp.zeros_like(k_rev)], axis=1)
    for t in range(CHUNK):
        off = (CHUNK - 1 - t) * SSM_GROUP
        m_ref[0, t * SSM_GROUP:(t + 1) * SSM_GROUP, :] = k_ext[:, off:off + kdim].astype(m_ref.dtype)

    are_r, aim_r = discretise(lrr_ref[0], lir_ref[0])
    cre2, cim2 = cre2_ref[0], cim2_ref[0]
    first = lax.broadcasted_iota(jnp.int32, (1, 2 * p_), 1) < p_
    pr, pi = are_r, aim_r
    for t in range(CHUNK):
        blk_v = jnp.where(first, cre2 * pr - cim2 * pi, -(cre2 * pi + cim2 * pr))
        v_ref[0, t * SSM_GROUP:(t + 1) * SSM_GROUP, :] = blk_v.astype(v_ref.dtype)
        pr, pi = _cmul(pr, pi, are_r, aim_r)

    lane = lax.broadcasted_iota(jnp.int32, (1, 2 * p_), 1)
    qr, qi = pow_re[CHUNK], pow_im[CHUNK]
    out = jnp.zeros((p_, 2 * p_), F32)
    for k in range(n_levels):
        out = jnp.where(lane == k, qr, out)
        out = jnp.where(lane == p_ + k, qi, out)
        qr, qi = _cmul(qr, qi, qr, qi)
    ap_ref[0] = out


def _s5_prep(log_dt, lam_re, lam_im, b_re, b_im, c_re, c_im, *, n_levels):
    g, p_ = lam_re.shape
    kdim = CHUNK * SSM_GROUP
    ldt = log_dt.reshape(g, 1, 1)
    lrc, lic = lam_re.reshape(g, p_, 1), lam_im.reshape(g, p_, 1)
    lrr = jnp.concatenate([lam_re, lam_re], axis=-1).reshape(g, 1, 2 * p_)
    lir = jnp.concatenate([lam_im, lam_im], axis=-1).reshape(g, 1, 2 * p_)
    bt_re, bt_im = jnp.tile(b_re, (1, 1, CHUNK)), jnp.tile(b_im, (1, 1, CHUNK))
    cre2 = jnp.concatenate([c_re, c_re], axis=-1)
    cim2 = jnp.concatenate([c_im, c_im], axis=-1)

    def spec(*shape):
        return pl.BlockSpec((1,) + shape, lambda i: (i, 0, 0))

    return pl.pallas_call(
        functools.partial(_s5prep_kernel, n_levels=n_levels),
        grid=(g,),
        in_specs=[spec(1, 1), spec(p_, 1), spec(p_, 1), spec(1, 2 * p_), spec(1, 2 * p_),
                  spec(p_, kdim), spec(p_, kdim), spec(SSM_GROUP, p_), spec(SSM_GROUP, p_),
                  spec(SSM_GROUP, 2 * p_), spec(SSM_GROUP, 2 * p_)],
        out_specs=[spec(kdim, kdim), spec(2 * p_, kdim), spec(kdim, 2 * p_), spec(p_, 2 * p_)],
        out_shape=[jax.ShapeDtypeStruct((g, kdim, kdim), BF16),
                   jax.ShapeDtypeStruct((g, 2 * p_, kdim), BF16),
                   jax.ShapeDtypeStruct((g, kdim, 2 * p_), BF16),
                   jax.ShapeDtypeStruct((g, p_, 2 * p_), F32)],
        compiler_params=_cparams("arbitrary"),
        name="s5prep",
    )(ldt, lrc, lic, lrr, lir, bt_re, bt_im, c_re, c_im, cre2, cim2)


def _s5_kernel(x_ref, m_ref, w_ref, v_ref, ap_ref, d_ref, o_ref, *, n_levels):
    p_ = SSM_STATE
    kdim = CHUNK * SSM_GROUP
    nc = x_ref.shape[-1]
    x = x_ref[...].reshape(kdim, nc)
    xb = x.astype(BF16)
    y = jnp.dot(m_ref[0], xb, preferred_element_type=F32)
    loc = jnp.dot(w_ref[0], xb, preferred_element_type=F32)
    s_re, s_im = loc[:p_], loc[p_:]
    lane = lax.broadcasted_iota(jnp.int32, (1, nc), 1)
    ap = ap_ref[0]
    for k in range(n_levels):
        sh = 1 << k
        pr, pi = ap[:, k:k + 1], ap[:, p_ + k:p_ + k + 1]
        keep = lane >= sh
        t_re = jnp.where(keep, pltpu.roll(s_re, sh, 1), 0.0)
        t_im = jnp.where(keep, pltpu.roll(s_im, sh, 1), 0.0)
        s_re, s_im = s_re + (pr * t_re - pi * t_im), s_im + (pr * t_im + pi * t_re)
    keep = lane >= 1
    start = jnp.concatenate([jnp.where(keep, pltpu.roll(s_re, 1, 1), 0.0),
                             jnp.where(keep, pltpu.roll(s_im, 1, 1), 0.0)], axis=0)
    y = y + jnp.dot(v_ref[0], start.astype(BF16), preferred_element_type=F32)
    y = y + d_ref[0] * x
    o_ref[...] = jax.nn.gelu(y).astype(o_ref.dtype).reshape(o_ref.shape)


def _s5_scan(u_t, m_op, w_op, v_op, a_pow, d_col, *, n_levels):
    _, width, nc = u_t.shape
    g = width // SSM_GROUP
    kdim = CHUNK * SSM_GROUP
    p2 = 2 * SSM_STATE

    def spec(*shape):
        return pl.BlockSpec((1,) + shape, lambda i: (i, 0, 0))

    io = pl.BlockSpec((CHUNK, SSM_GROUP, nc), lambda i: (0, i, 0))
    return pl.pallas_call(
        functools.partial(_s5_kernel, n_levels=n_levels),
        grid=(g,),
        in_specs=[io, spec(kdim, kdim), spec(p2, kdim), spec(kdim, p2), spec(SSM_STATE, p2), spec(kdim, 1)],
        out_specs=io,
        out_shape=jax.ShapeDtypeStruct((CHUNK, width, nc), BF16),
        compiler_params=_cparams("arbitrary"),
        name="s5",
    )(u_t, m_op, w_op, v_op, a_pow, d_col)


def _attn_kernel(sink_ref, q_ref, kc_ref, kp_ref, vc_ref, vp_ref, ga_ref, o_ref, *, n_groups):
    i = pl.program_id(0)
    gh = pl.program_id(1)
    tq = q_ref.shape[0]
    nb = tq // WINDOW
    scale = HEAD_DIM ** -0.5
    row = lax.broadcasted_iota(jnp.int32, (WINDOW, WINDOW), 0)
    col = lax.broadcasted_iota(jnp.int32, (WINDOW, WINDOW), 1)
    tri1 = col <= row
    tri = jnp.concatenate([tri1] * REP, axis=0)
    nt = (((1,), (1,)), ((), ()))
    for b in range(nb):
        rows = slice(b * WINDOW, (b + 1) * WINDOW)
        has_prev = (i * nb + b) > 0
        valid = tri | has_prev
        for g in range(n_groups):
            kcols = slice(g * HEAD_DIM, (g + 1) * HEAD_DIM)
            if b == 0:
                k_prev, v_prev = kp_ref[:, kcols], vp_ref[:, kcols]
            else:
                prows = slice((b - 1) * WINDOW, b * WINDOW)
                k_prev, v_prev = kc_ref[prows, kcols], vc_ref[prows, kcols]
            k_band = jnp.concatenate([k_prev, kc_ref[rows, kcols]], axis=0)
            v_band = jnp.concatenate([v_prev, vc_ref[rows, kcols]], axis=0)
            qcols = [slice((g * REP + r) * HEAD_DIM, (g * REP + r + 1) * HEAD_DIM) for r in range(REP)]
            qg = jnp.concatenate([q_ref[rows, c] for c in qcols], axis=0)
            s2 = lax.dot_general(qg, k_band, nt, preferred_element_type=F32)
            s = jnp.where(tri, s2[:, WINDOW:], s2[:, :WINDOW]) * scale
            s = jnp.where(valid, s, MASK_VALUE)
            probs = []
            for r in range(REP):
                sink = sink_ref[(gh * n_groups + g) * REP + r]
                sr = s[r * WINDOW:(r + 1) * WINDOW]
                m = jnp.maximum(jnp.max(sr, axis=-1, keepdims=True), sink)
                p = jnp.exp(sr - m)
                denom = jnp.sum(p, axis=-1, keepdims=True) + jnp.exp(sink - m)
                probs.append(p * (1.0 / denom))
            pr = jnp.concatenate(probs, axis=0)
            p2 = jnp.concatenate([jnp.where(tri, 0.0, pr), jnp.where(tri, pr, 0.0)], axis=1).astype(BF16)
            o = jnp.dot(p2, v_band, preferred_element_type=F32)
            for r in range(REP):
                gate = ga_ref[rows, qcols[r]].astype(F32)
                o_ref[rows, qcols[r]] = (gate * o[r * WINDOW:(r + 1) * WINDOW]).astype(o_ref.dtype)


def _attention(sinks, pz, *, d_model, kv_width, tq=256, n_groups=4):
    s = pz.shape[0]
    tq = min(tq, s)
    nb = tq // WINDOW
    wq = n_groups * REP * HEAD_DIM
    wk = n_groups * HEAD_DIM
    k0 = d_model // wk
    v0 = (d_model + kv_width) // wk
    a0 = (d_model + 2 * kv_width) // wq

    def prev_map(col0):
        return lambda i, gh: (jnp.maximum(i * nb - 1, 0), col0 + gh)

    return pl.pallas_call(
        functools.partial(_attn_kernel, n_groups=n_groups),
        grid=(s // tq, d_model // wq),
        in_specs=[
            pl.BlockSpec(memory_space=pltpu.SMEM),
            pl.BlockSpec((tq, wq), lambda i, gh: (i, gh)),
            pl.BlockSpec((tq, wk), lambda i, gh: (i, k0 + gh)),
            pl.BlockSpec((WINDOW, wk), prev_map(k0)),
            pl.BlockSpec((tq, wk), lambda i, gh: (i, v0 + gh)),
            pl.BlockSpec((WINDOW, wk), prev_map(v0)),
            pl.BlockSpec((tq, wq), lambda i, gh: (i, a0 + gh)),
        ],
        out_specs=pl.BlockSpec((tq, wq), lambda i, gh: (i, gh)),
        out_shape=jax.ShapeDtypeStruct((s, d_model), BF16),
        compiler_params=_cparams("arbitrary", "arbitrary"),
        name="attn",
    )(sinks, pz, pz, pz, pz, pz, pz)


def _glu_kernel(y_ref, wa_ref, wb_ref, at_ref, gs_ref, o_ref):
    tn_ = (((0,), (0,)), ((), ()))
    y = y_ref[0]
    a = lax.dot_general(y, wa_ref[...].astype(BF16), tn_, preferred_element_type=F32)
    b = lax.dot_general(y, wb_ref[...].astype(BF16), tn_, preferred_element_type=F32)
    branch = a * jax.nn.sigmoid(b)
    o_ref[...] = (at_ref[...].astype(F32) + gs_ref[...].astype(F32) * branch).astype(o_ref.dtype)


def _glu_merge(y_t, w_glu, attn_phase, pz_phase, *, d_model, pz_width, gs_col0, tn=512):
    _, width, nc = y_t.shape
    nj = d_model // tn
    return pl.pallas_call(
        _glu_kernel,
        grid=(nj, CHUNK),
        in_specs=[
            pl.BlockSpec((1, width, nc), lambda j, t: (t, 0, 0)),
            pl.BlockSpec((width, tn), lambda j, t: (0, j)),
            pl.BlockSpec((width, tn), lambda j, t: (0, nj + j)),
            pl.BlockSpec((nc, tn), lambda j, t: (0, t * nj + j)),
            pl.BlockSpec((nc, tn), lambda j, t: (0, (t * pz_width + gs_col0) // tn + j)),
        ],
        out_specs=pl.BlockSpec((nc, tn), lambda j, t: (0, t * nj + j)),
        out_shape=jax.ShapeDtypeStruct((nc, CHUNK * d_model), BF16),
        compiler_params=_cparams("arbitrary", "arbitrary"),
        name="glu",
    )(y_t, w_glu, w_glu, attn_phase, pz_phase)


def _resid_mm_kernel(a_ref, w_ref, x_ref, g_ref, o_ref):
    acc = jnp.dot(a_ref[...], w_ref[...].astype(BF16), preferred_element_type=F32)
    o_ref[...] = x_ref[...] + g_ref[...] * acc


def _resid_matmul(a, w, x, mod, gate_idx, *, tm=1024, tn=512):
    s, k = a.shape
    n = w.shape[1]
    tm = min(tm, s)
    return pl.pallas_call(
        _resid_mm_kernel,
        grid=(s // tm, n // tn),
        in_specs=[
            pl.BlockSpec((tm, k), lambda i, j: (i, 0)),
            pl.BlockSpec((k, tn), lambda i, j: (0, j)),
            pl.BlockSpec((tm, tn), lambda i, j: (i, j)),
            pl.BlockSpec((1, tn), lambda i, j: (0, gate_idx * (n // tn) + j)),
        ],
        out_specs=pl.BlockSpec((tm, tn), lambda i, j: (i, j)),
        out_shape=jax.ShapeDtypeStruct((s, n), F32),
        compiler_params=_cparams("arbitrary", "arbitrary"),
        name="outp",
    )(a, w, x, mod)


def _ff1_kernel(h_ref, w_ref, o_ref):
    acc = jnp.dot(h_ref[...], w_ref[...].astype(BF16), preferred_element_type=F32)
    r = jnp.maximum(acc, 0.0)
    o_ref[...] = (r * r).astype(o_ref.dtype)


def _ff1(h, w, *, tm=1024, tn=512):
    s, k = h.shape
    n = w.shape[1]
    tm = min(tm, s)
    return pl.pallas_call(
        _ff1_kernel,
        grid=(s // tm, n // tn),
        in_specs=[pl.BlockSpec((tm, k), lambda i, j: (i, 0)), pl.BlockSpec((k, tn), lambda i, j: (0, j))],
        out_specs=pl.BlockSpec((tm, tn), lambda i, j: (i, j)),
        out_shape=jax.ShapeDtypeStruct((s, n), BF16),
        compiler_params=_cparams("arbitrary", "arbitrary"),
        name="ff1",
    )(h, w)


def _ff2_kernel(a_ref, w_ref, x_ref, g_ref, o_ref, acc_ref):
    kk = pl.program_id(2)

    @pl.when(kk == 0)
    def _():
        acc_ref[...] = jnp.zeros_like(acc_ref)

    acc_ref[...] += jnp.dot(a_ref[...], w_ref[...].astype(BF16), preferred_element_type=F32)

    @pl.when(kk == pl.num_programs(2) - 1)
    def _():
        o_ref[...] = x_ref[...] + g_ref[...] * acc_ref[...]


def _ff2(a, w, x, mod, gate_idx, *, tm=1024, tn=1024, tk=2048):
    s, k = a.shape
    n = w.shape[1]
    tm = min(tm, s)
    return pl.pallas_call(
        _ff2_kernel,
        grid=(s // tm, n // tn, k // tk),
        in_specs=[
            pl.BlockSpec((tm, tk), lambda i, j, kk: (i, kk)),
            pl.BlockSpec((tk, tn), lambda i, j, kk: (kk, j)),
            pl.BlockSpec((tm, tn), lambda i, j, kk: (i, j)),
            pl.BlockSpec((1, tn), lambda i, j, kk: (0, gate_idx * (n // tn) + j)),
        ],
        out_specs=pl.BlockSpec((tm, tn), lambda i, j, kk: (i, j)),
        out_shape=jax.ShapeDtypeStruct((s, n), F32),
        scratch_shapes=[pltpu.VMEM((tm, tn), F32)],
        compiler_params=_cparams("arbitrary", "arbitrary", "arbitrary"),
        name="ff2",
    )(a, w, x, mod)


def _layer(x, c_col, pos_col, w_ada, b_ada, norm1_g, norm2_g, w_in, q_norm_g, k_norm_g, attn_sinks,
           lam_re, lam_im, log_dt, b_re, b_im, c_re, c_im, ssm_d, w_glu, w_out, w_ff1, w_ff2):
    s, d = x.shape
    kv_width = N_KV_HEADS * HEAD_DIM
    ssm_width = ssm_d.shape[0]
    n_groups = ssm_width // SSM_GROUP
    nc = s // CHUNK
    n_levels = max(1, (nc - 1).bit_length())
    u_col0 = d + 2 * kv_width
    pz_width = 3 * d + 2 * kv_width

    mod = _modulation(c_col, w_ada, b_ada.reshape(1, -1))
    h = _norm_mod(x, norm1_g.reshape(1, d), mod, 0)
    rope_c, rope_sa, rope_sb = _rope_tables(pos_col)
    pz = _proj(h, w_in, q_norm_g.reshape(1, -1), k_norm_g.reshape(1, -1), rope_c, rope_sa, rope_sb,
               d_model=d, kv_width=kv_width, ssm_width=ssm_width)
    u_t = _proj_u(h.reshape(nc, CHUNK * d), w_in, d=d, col0=u_col0, ssm_width=ssm_width)

    m_op, w_op, v_op, a_pow = _s5_prep(log_dt, lam_re, lam_im, b_re, b_im, c_re, c_im, n_levels=n_levels)
    d_col = jnp.tile(ssm_d.reshape(n_groups, 1, SSM_GROUP), (1, CHUNK, 1)).reshape(n_groups, CHUNK * SSM_GROUP, 1)
    y_t = _s5_scan(u_t, m_op, w_op, v_op, a_pow, d_col, n_levels=n_levels)

    attn = _attention(attn_sinks, pz, d_model=d, kv_width=kv_width)
    merged = _glu_merge(y_t, w_glu, attn.reshape(nc, CHUNK * d), pz.reshape(nc, CHUNK * pz_width),
                        d_model=d, pz_width=pz_width, gs_col0=pz_width - d).reshape(s, d)
    x1 = _resid_matmul(merged, w_out, x, mod, 2)

    h2 = _norm_mod(x1, norm2_g.reshape(1, d), mod, 3)
    act = _ff1(h2, w_ff1)
    return _ff2(act, w_ff2, x1, mod, 5)


def kernel(x, c, positions, w_ada, b_ada, norm1_g, norm2_g, w_in, q_norm_g, k_norm_g, attn_sinks, ssm_lam_re,
           ssm_lam_im, ssm_log_dt, ssm_b_re, ssm_b_im, ssm_c_re, ssm_c_im, ssm_d, w_glu, w_out, w_ff1, w_ff2):
    bsz, s, d = x.shape
    assert bsz == 1 and d == N_Q_HEADS * HEAD_DIM and s % (CHUNK * LANES) == 0
    xs = x[0]
    c_col = c.reshape(d, 1)
    pos_col = positions.reshape(s, 1)
    for l in range(w_ada.shape[0]):
        xs = _layer(xs, c_col, pos_col, w_ada[l], b_ada[l], norm1_g[l], norm2_g[l], w_in[l], q_norm_g[l],
                    k_norm_g[l], attn_sinks[l], ssm_lam_re[l], ssm_lam_im[l], ssm_log_dt[l], ssm_b_re[l],
                    ssm_b_im[l], ssm_c_re[l], ssm_c_im[l], ssm_d[l], w_glu[l], w_out[l], w_ff1[l], w_ff2[l])
    return xs[None]
```

```python
import functools
import math

import jax
import jax.numpy as jnp
from jax import lax
from jax.experimental import pallas as pl
from jax.experimental.pallas import tpu as pltpu

F32 = jnp.float32
BF16 = jnp.bfloat16

LANES = 128
SUBLANES = 8
MXU_DIM = 256
VMEM_LIMIT_BYTES = 56 * 1024 * 1024

N_Q_HEADS = 32
N_KV_HEADS = 8
HEAD_DIM = 128
REP = N_Q_HEADS // N_KV_HEADS
WINDOW = 128
ROPE_DIM = HEAD_DIM // 4
ROPE_HALF = ROPE_DIM // 2
ROPE_THETA = 500000.0
SSM_GROUP = 16
SSM_STATE = 64
N_MOD = 6
EPS = 1e-6
MASK_VALUE = -1e30
CHUNK = MXU_DIM // SSM_GROUP


def _phase_perm():
    n = CHUNK * CHUNK
    r = lax.broadcasted_iota(jnp.int32, (n, n), 0)
    c = lax.broadcasted_iota(jnp.int32, (n, n), 1)
    return jnp.where((r % CHUNK) * CHUNK + r // CHUNK == c, 1.0, 0.0).astype(BF16)


def _to_phase_major(v, perm):
    return jnp.dot(perm, v, preferred_element_type=F32).astype(v.dtype).reshape(CHUNK, CHUNK, v.shape[-1])


def _cparams(*sem):
    return pltpu.CompilerParams(dimension_semantics=sem, vmem_limit_bytes=VMEM_LIMIT_BYTES)


def _mod_kernel(c_ref, w_ref, b_ref, o_ref, *, rk):
    d, tn = w_ref.shape

    def body(r, acc):
        rows = pl.ds(pl.multiple_of(r * rk, rk), rk)
        c = c_ref[rows, :]
        ca = c * jax.nn.sigmoid(c)
        prod = w_ref[rows, :] * ca
        return acc + jnp.sum(prod.reshape(rk // SUBLANES, SUBLANES, tn), axis=0)

    acc = lax.fori_loop(0, d // rk, body, jnp.zeros((SUBLANES, tn), F32))
    o_ref[...] = jnp.sum(acc, axis=0, keepdims=True) + b_ref[...]


def _modulation(c_col, w_ada, b_ada_row, *, tn=512, rk=512):
    d, n = w_ada.shape
    return pl.pallas_call(
        functools.partial(_mod_kernel, rk=rk),
        grid=(n // tn,),
        in_specs=[
            pl.BlockSpec((d, 1), lambda j: (0, 0)),
            pl.BlockSpec((d, tn), lambda j: (0, j)),
            pl.BlockSpec((1, tn), lambda j: (0, j)),
        ],
        out_specs=pl.BlockSpec((1, tn), lambda j: (0, j)),
        out_shape=jax.ShapeDtypeStruct((1, n), F32),
        compiler_params=_cparams("arbitrary"),
        name="mod",
    )(c_col, w_ada, b_ada_row)


def _norm_kernel(x_ref, g_ref, sh_ref, sc_ref, o_ref, *phase_ref):
    def normed(x):
        ms = jnp.mean(x * x, axis=-1, keepdims=True)
        y = x * lax.rsqrt(ms + EPS) * g_ref[...]
        return (y * (1.0 + sc_ref[...]) + sh_ref[...]).astype(o_ref.dtype)

    hb = normed(x_ref[...])
    o_ref[...] = hb
    for oph_ref in phase_ref:
        perm = _phase_perm()
        grp = CHUNK * CHUNK
        for q in range(hb.shape[0] // grp):
            oph_ref[:, q * CHUNK:(q + 1) * CHUNK, :] = _to_phase_major(hb[q * grp:(q + 1) * grp], perm)


def _norm_mod(x, g_row, mod, shift_idx, *, with_phase_major=False, tm=256):
    s, d = x.shape
    out_specs = [pl.BlockSpec((tm, d), lambda i: (i, 0))]
    out_shape = [jax.ShapeDtypeStruct((s, d), BF16)]
    if with_phase_major:
        out_specs.append(pl.BlockSpec((CHUNK, tm // CHUNK, d), lambda i: (0, i, 0)))
        out_shape.append(jax.ShapeDtypeStruct((CHUNK, s // CHUNK, d), BF16))
    return pl.pallas_call(
        _norm_kernel,
        grid=(s // tm,),
        in_specs=[
            pl.BlockSpec((tm, d), lambda i: (i, 0)),
            pl.BlockSpec((1, d), lambda i: (0, 0)),
            pl.BlockSpec((1, d), lambda i: (0, shift_idx)),
            pl.BlockSpec((1, d), lambda i: (0, shift_idx + 1)),
        ],
        out_specs=out_specs,
        out_shape=out_shape,
        compiler_params=_cparams("arbitrary"),
        name="norm",
    )(x, g_row, mod, mod)


def _rope_kernel(pos_ref, c_ref, sa_ref, sb_ref):
    pos = pos_ref[...].astype(F32)
    lane = lax.broadcasted_iota(jnp.int32, (1, LANES), 1)
    idx = (lane & (ROPE_HALF - 1)).astype(F32)
    inv_freq = jnp.power(jnp.float32(ROPE_THETA), -2.0 * idx / ROPE_DIM)
    ang = pos * inv_freq
    cos = jnp.cos(ang)
    sin = jnp.sin(ang)
    c_ref[...] = jnp.where(lane < ROPE_DIM, cos, 1.0)
    sa_ref[...] = jnp.where(lane < ROPE_HALF, -sin, 0.0)
    sb_ref[...] = jnp.where((lane >= ROPE_HALF) & (lane < ROPE_DIM), sin, 0.0)


def _rope_tables(pos_col, *, tm=1024):
    s = pos_col.shape[0]
    tm = min(tm, s)
    spec = pl.BlockSpec((tm, LANES), lambda i: (i, 0))
    shp = jax.ShapeDtypeStruct((s, LANES), F32)
    return pl.pallas_call(
        _rope_kernel,
        grid=(s // tm,),
        in_specs=[pl.BlockSpec((tm, 1), lambda i: (i, 0))],
        out_specs=[spec, spec, spec],
        out_shape=[shp, shp, shp],
        compiler_params=_cparams("arbitrary"),
        name="rope",
    )(pos_col)


def _proj_kernel(h_ref, w_ref, qg_ref, kg_ref, rc_ref, rsa_ref, rsb_ref, o_ref, acc_ref, *, n_q, n_qk, n_qkv):
    j = pl.program_id(1)
    tn = o_ref.shape[1]
    acc_ref[...] = jnp.dot(h_ref[...], w_ref[...].astype(BF16), preferred_element_type=F32)

    @pl.when(j < n_qk)
    def _():
        g = jnp.where(j < n_q, qg_ref[...], kg_ref[...])
        c, sa, sb = rc_ref[...], rsa_ref[...], rsb_ref[...]
        for hh in range(tn // HEAD_DIM):
            cols = slice(hh * HEAD_DIM, (hh + 1) * HEAD_DIM)
            xh = acc_ref[:, cols]
            ms = jnp.mean(xh * xh, axis=-1, keepdims=True)
            y = xh * lax.rsqrt(ms + EPS) * g
            y = y * c + pltpu.roll(y, HEAD_DIM - ROPE_HALF, 1) * sa + pltpu.roll(y, ROPE_HALF, 1) * sb
            o_ref[:, cols] = y.astype(o_ref.dtype)

    @pl.when((j >= n_qk) & (j < n_qkv))
    def _():
        o_ref[...] = acc_ref[...].astype(o_ref.dtype)

    @pl.when(j >= n_qkv)
    def _():
        o_ref[...] = jax.nn.sigmoid(acc_ref[...]).astype(o_ref.dtype)


def _proj(h, w_in, qg_row, kg_row, rope_c, rope_sa, rope_sb, *, d_model, kv_width, ssm_width, tm=1024, tn=512):
    s, d = h.shape
    tm = min(tm, s)
    n_q = d_model // tn
    n_qk = n_q + kv_width // tn
    n_qkv = n_qk + kv_width // tn
    n_u = ssm_width // tn
    n_out = n_qkv + d_model // tn
    tab = pl.BlockSpec((tm, LANES), lambda i, j: (i, 0))
    row = pl.BlockSpec((1, HEAD_DIM), lambda i, j: (0, 0))
    return pl.pallas_call(
        functools.partial(_proj_kernel, n_q=n_q, n_qk=n_qk, n_qkv=n_qkv),
        grid=(s // tm, n_out),
        in_specs=[
            pl.BlockSpec((tm, d), lambda i, j: (i, 0)),
            pl.BlockSpec((d, tn), lambda i, j: (0, jnp.where(j < n_qkv, j, j + n_u))),
            row, row, tab, tab, tab,
        ],
        out_specs=pl.BlockSpec((tm, tn), lambda i, j: (i, j)),
        out_shape=jax.ShapeDtypeStruct((s, n_out * tn), BF16),
        scratch_shapes=[pltpu.VMEM((tm, tn), F32)],
        compiler_params=_cparams("arbitrary", "arbitrary"),
        name="proj",
    )(h, w_in, qg_row, kg_row, rope_c, rope_sa, rope_sb)


def _proju_kernel(h_ref, w_ref, ut_ref, gs_ref, *, n_u):
    j = pl.program_id(1)
    wb = w_ref[...].astype(BF16)

    @pl.when(j < n_u)
    def _():
        for p in range(h_ref.shape[0]):
            ut_ref[p] = jnp.dot(h_ref[p], wb, preferred_element_type=F32).T

    @pl.when(j >= n_u)
    def _():
        for p in range(h_ref.shape[0]):
            gs_ref[p] = jax.nn.sigmoid(jnp.dot(h_ref[p], wb, preferred_element_type=F32)).astype(gs_ref.dtype)


def _proj_u(h_phase, w_in, *, u_col0, gs_col0, ssm_width, d_model, phases=2, tn=512):
    _, nc, d = h_phase.shape
    n_u = ssm_width // tn
    n_gs = d_model // tn
    return pl.pallas_call(
        functools.partial(_proju_kernel, n_u=n_u),
        grid=(CHUNK // phases, n_u + n_gs),
        in_specs=[
            pl.BlockSpec((phases, nc, d), lambda t, j: (t, 0, 0)),
            pl.BlockSpec((d, tn), lambda t, j: (0, jnp.where(j < n_u, u_col0 // tn + j, gs_col0 // tn + j - n_u))),
        ],
        out_specs=[
            pl.BlockSpec((phases, tn, nc), lambda t, j: (t, jnp.minimum(j, n_u - 1), 0)),
            pl.BlockSpec((phases, nc, tn), lambda t, j: (t, 0, jnp.maximum(j - n_u, 0))),
        ],
        out_shape=[jax.ShapeDtypeStruct((CHUNK, ssm_width, nc), F32),
                   jax.ShapeDtypeStruct((CHUNK, nc, d_model), BF16)],
        compiler_params=_cparams("arbitrary", "arbitrary"),
        name="proj_u",
    )(h_phase, w_in)


def _cmul(ar, ai, br, bi):
    return ar * br - ai * bi, ar * bi + ai * br


def _s5prep_kernel(ldt_ref, lrc_ref, lic_ref, lrr_ref, lir_ref, bre_ref, bim_ref, cre_ref, cim_ref,
                   cre2_ref, cim2_ref, m_ref, w_ref, v_ref, ap_ref, *, n_levels):
    p_ = SSM_STATE
    kdim = CHUNK * SSM_GROUP
    dt = jnp.exp(ldt_ref[0])

    def discretise(lr, li):
        mag = jnp.exp(lr * dt)
        return mag * jnp.cos(li * dt), mag * jnp.sin(li * dt)

    lr, li = lrc_ref[0], lic_ref[0]
    are, aim = discretise(lr, li)
    den = lr * lr + li * li
    nr = are - 1.0
    coef_re = (nr * lr + aim * li) / den
    coef_im = (aim * lr - nr * li) / den
    bt_re, bt_im = bre_ref[0], bim_ref[0]
    bbar_re = coef_re * bt_re - coef_im * bt_im
    bbar_im = coef_re * bt_im + coef_im * bt_re

    pow_re = [jnp.ones_like(are)]
    pow_im = [jnp.zeros_like(are)]
    for _ in range(CHUNK):
        nre, nim = _cmul(pow_re[-1], pow_im[-1], are, aim)
        pow_re.append(nre)
        pow_im.append(nim)

    blk = lax.broadcasted_iota(jnp.int32, (1, kdim), 1) // SSM_GROUP
    tab_re = jnp.zeros((p_, kdim), F32)
    tab_im = jnp.zeros((p_, kdim), F32)
    for m in range(CHUNK):
        sel = blk == m
        tab_re = jnp.where(sel, pow_re[CHUNK - 1 - m], tab_re)
        tab_im = jnp.where(sel, pow_im[CHUNK - 1 - m], tab_im)
    g_re, g_im = _cmul(tab_re, tab_im, bbar_re, bbar_im)
    w_ref[0] = jnp.concatenate([g_re, g_im], axis=0).astype(w_ref.dtype)

    hi = lax.Precision.HIGHEST
    k_rev = (jnp.dot(cre_ref[0], g_re, precision=hi, preferred_element_type=F32)
             - jnp.dot(cim_ref[0], g_im, precision=hi, preferred_element_type=F32))
    k_ext = jnp.concatenate([k_rev, jnp.zeros_like(k_rev)], axis=1)
    for t in range(CHUNK):
        off = (CHUNK - 1 - t) * SSM_GROUP
        m_ref[0, t * SSM_GROUP:(t + 1) * SSM_GROUP, :] = k_ext[:, off:off + kdim].astype(m_ref.dtype)

    are_r, aim_r = discretise(lrr_ref[0], lir_ref[0])
    cre2, cim2 = cre2_ref[0], cim2_ref[0]
    first = lax.broadcasted_iota(jnp.int32, (1, 2 * p_), 1) < p_
    pr, pi = are_r, aim_r
    for t in range(CHUNK):
        blk_v = jnp.where(first, cre2 * pr - cim2 * pi, -(cre2 * pi + cim2 * pr))
        v_ref[0, t * SSM_GROUP:(t + 1) * SSM_GROUP, :] = blk_v.astype(v_ref.dtype)
        pr, pi = _cmul(pr, pi, are_r, aim_r)

    lane = lax.broadcasted_iota(jnp.int32, (1, 2 * p_), 1)
    qr, qi = pow_re[CHUNK], pow_im[CHUNK]
    out = jnp.zeros((p_, 2 * p_), F32)
    for k in range(n_levels):
        out = jnp.where(lane == k, qr, out)
        out = jnp.where(lane == p_ + k, qi, out)
        qr, qi = _cmul(qr, qi, qr, qi)
    ap_ref[0] = out


def _s5_prep(log_dt, lam_re, lam_im, b_re, b_im, c_re, c_im, *, n_levels):
    g, p_ = lam_re.shape
    kdim = CHUNK * SSM_GROUP
    ldt = log_dt.reshape(g, 1, 1)
    lrc, lic = lam_re.reshape(g, p_, 1), lam_im.reshape(g, p_, 1)
    lrr = jnp.concatenate([lam_re, lam_re], axis=-1).reshape(g, 1, 2 * p_)
    lir = jnp.concatenate([lam_im, lam_im], axis=-1).reshape(g, 1, 2 * p_)
    bt_re, bt_im = jnp.tile(b_re, (1, 1, CHUNK)), jnp.tile(b_im, (1, 1, CHUNK))
    cre2 = jnp.concatenate([c_re, c_re], axis=-1)
    cim2 = jnp.concatenate([c_im, c_im], axis=-1)

    def spec(*shape):
        return pl.BlockSpec((1,) + shape, lambda i: (i, 0, 0))

    return pl.pallas_call(
        functools.partial(_s5prep_kernel, n_levels=n_levels),
        grid=(g,),
        in_specs=[spec(1, 1), spec(p_, 1), spec(p_, 1), spec(1, 2 * p_), spec(1, 2 * p_),
                  spec(p_, kdim), spec(p_, kdim), spec(SSM_GROUP, p_), spec(SSM_GROUP, p_),
                  spec(SSM_GROUP, 2 * p_), spec(SSM_GROUP, 2 * p_)],
        out_specs=[spec(kdim, kdim), spec(2 * p_, kdim), spec(kdim, 2 * p_), spec(p_, 2 * p_)],
        out_shape=[jax.ShapeDtypeStruct((g, kdim, kdim), BF16),
                   jax.ShapeDtypeStruct((g, 2 * p_, kdim), BF16),
                   jax.ShapeDtypeStruct((g, kdim, 2 * p_), BF16),
                   jax.ShapeDtypeStruct((g, p_, 2 * p_), F32)],
        compiler_params=_cparams("arbitrary"),
        name="s5prep",
    )(ldt, lrc, lic, lrr, lir, bt_re, bt_im, c_re, c_im, cre2, cim2)


def _s5_kernel(x_ref, m_ref, w_ref, v_ref, ap_ref, d_ref, o_ref, *, n_levels):
    p_ = SSM_STATE
    kdim = CHUNK * SSM_GROUP
    nc = x_ref.shape[-1]
    x = x_ref[...].reshape(kdim, nc)
    xb = x.astype(BF16)
    y = jnp.dot(m_ref[0], xb, preferred_element_type=F32)
    loc = jnp.dot(w_ref[0], xb, preferred_element_type=F32)
    s_re, s_im = loc[:p_], loc[p_:]
    lane = lax.broadcasted_iota(jnp.int32, (1, nc), 1)
    ap = ap_ref[0]
    for k in range(n_levels):
        sh = 1 << k
        pr, pi = ap[:, k:k + 1], ap[:, p_ + k:p_ + k + 1]
        keep = lane >= sh
        t_re = jnp.where(keep, pltpu.roll(s_re, sh, 1), 0.0)
        t_im = jnp.where(keep, pltpu.roll(s_im, sh, 1), 0.0)
        s_re, s_im = s_re + (pr * t_re - pi * t_im), s_im + (pr * t_im + pi * t_re)
    keep = lane >= 1
    start = jnp.concatenate([jnp.where(keep, pltpu.roll(s_re, 1, 1), 0.0),
                             jnp.where(keep, pltpu.roll(s_im, 1, 1), 0.0)], axis=0)
    y = y + jnp.dot(v_ref[0], start.astype(BF16), preferred_element_type=F32)
    y = y + d_ref[0] * x
    o_ref[...] = jax.nn.gelu(y).astype(o_ref.dtype).reshape(o_ref.shape)


def _s5_scan(u_t, m_op, w_op, v_op, a_pow, d_col, *, n_levels):
    _, width, nc = u_t.shape
    g = width // SSM_GROUP
    kdim = CHUNK * SSM_GROUP
    p2 = 2 * SSM_STATE

    def spec(*shape):
        return pl.BlockSpec((1,) + shape, lambda i: (i, 0, 0))

    io = pl.BlockSpec((CHUNK, SSM_GROUP, nc), lambda i: (0, i, 0))
    return pl.pallas_call(
        functools.partial(_s5_kernel, n_levels=n_levels),
        grid=(g,),
        in_specs=[io, spec(kdim, kdim), spec(p2, kdim), spec(kdim, p2), spec(SSM_STATE, p2), spec(kdim, 1)],
        out_specs=io,
        out_shape=jax.ShapeDtypeStruct((CHUNK, width, nc), BF16),
        compiler_params=_cparams("arbitrary"),
        name="s5",
    )(u_t, m_op, w_op, v_op, a_pow, d_col)


def _attn_kernel(sink_ref, q_ref, kc_ref, kp_ref, vc_ref, vp_ref, ga_ref, o_ref, nat_ref, *, n_groups):
    i = pl.program_id(0)
    gh = pl.program_id(1)
    tq = q_ref.shape[0]
    nb = tq // WINDOW
    scale = HEAD_DIM ** -0.5
    row = lax.broadcasted_iota(jnp.int32, (WINDOW, WINDOW), 0)
    col = lax.broadcasted_iota(jnp.int32, (WINDOW, WINDOW), 1)
    tri1 = col <= row
    tri = jnp.concatenate([tri1] * REP, axis=0)
    nt = (((1,), (1,)), ((), ()))
    for b in range(nb):
        rows = slice(b * WINDOW, (b + 1) * WINDOW)
        has_prev = (i * nb + b) > 0
        valid = tri | has_prev
        for g in range(n_groups):
            kcols = slice(g * HEAD_DIM, (g + 1) * HEAD_DIM)
            if b == 0:
                k_prev, v_prev = kp_ref[:, kcols], vp_ref[:, kcols]
            else:
                prows = slice((b - 1) * WINDOW, b * WINDOW)
                k_prev, v_prev = kc_ref[prows, kcols], vc_ref[prows, kcols]
            k_band = jnp.concatenate([k_prev, kc_ref[rows, kcols]], axis=0)
            v_band = jnp.concatenate([v_prev, vc_ref[rows, kcols]], axis=0)
            qcols = [slice((g * REP + r) * HEAD_DIM, (g * REP + r + 1) * HEAD_DIM) for r in range(REP)]
            qg = jnp.concatenate([q_ref[rows, c] for c in qcols], axis=0)
            s2 = lax.dot_general(qg, k_band, nt, preferred_element_type=F32)
            s = jnp.where(tri, s2[:, WINDOW:], s2[:, :WINDOW]) * scale
            s = jnp.where(valid, s, MASK_VALUE)
            probs = []
            for r in range(REP):
                sink = sink_ref[(gh * n_groups + g) * REP + r]
                sr = s[r * WINDOW:(r + 1) * WINDOW]
                m = jnp.maximum(jnp.max(sr, axis=-1, keepdims=True), sink)
                p = jnp.exp(sr - m)
                denom = jnp.sum(p, axis=-1, keepdims=True) + jnp.exp(sink - m)
                probs.append(p * (1.0 / denom))
            pr = jnp.concatenate(probs, axis=0)
            p2 = jnp.concatenate([jnp.where(tri, 0.0, pr), jnp.where(tri, pr, 0.0)], axis=1).astype(BF16)
            o = jnp.dot(p2, v_band, preferred_element_type=F32)
            for r in range(REP):
                gate = ga_ref[rows, qcols[r]].astype(F32)
                nat_ref[rows, qcols[r]] = gate * o[r * WINDOW:(r + 1) * WINDOW]
    perm = _phase_perm()
    grp = CHUNK * CHUNK
    for q in range(tq // grp):
        nat = nat_ref[q * grp:(q + 1) * grp, :].astype(o_ref.dtype)
        o_ref[:, q * CHUNK:(q + 1) * CHUNK, :] = _to_phase_major(nat, perm)


def _attention(sinks, pz, *, d_model, kv_width, tq=256, n_groups=4):
    s = pz.shape[0]
    tq = min(tq, s)
    nb = tq // WINDOW
    wq = n_groups * REP * HEAD_DIM
    wk = n_groups * HEAD_DIM
    k0 = d_model // wk
    v0 = (d_model + kv_width) // wk
    a0 = (d_model + 2 * kv_width) // wq

    def prev_map(col0):
        return lambda i, gh: (jnp.maximum(i * nb - 1, 0), col0 + gh)

    return pl.pallas_call(
        functools.partial(_attn_kernel, n_groups=n_groups),
        grid=(s // tq, d_model // wq),
        in_specs=[
            pl.BlockSpec(memory_space=pltpu.SMEM),
            pl.BlockSpec((tq, wq), lambda i, gh: (i, gh)),
            pl.BlockSpec((tq, wk), lambda i, gh: (i, k0 + gh)),
            pl.BlockSpec((WINDOW, wk), prev_map(k0)),
            pl.BlockSpec((tq, wk), lambda i, gh: (i, v0 + gh)),
            pl.BlockSpec((WINDOW, wk), prev_map(v0)),
            pl.BlockSpec((tq, wq), lambda i, gh: (i, a0 + gh)),
        ],
        out_specs=pl.BlockSpec((CHUNK, tq // CHUNK, wq), lambda i, gh: (0, i, gh)),
        out_shape=jax.ShapeDtypeStruct((CHUNK, s // CHUNK, d_model), BF16),
        scratch_shapes=[pltpu.VMEM((tq, wq), F32)],
        compiler_params=_cparams("arbitrary", "arbitrary"),
        name="attn",
    )(sinks, pz, pz, pz, pz, pz, pz)


def _glu_kernel(y_ref, wa_ref, wb_ref, at_ref, gs_ref, o_ref):
    tn_ = (((0,), (0,)), ((), ()))
    y = y_ref[0]
    a = lax.dot_general(y, wa_ref[...].astype(BF16), tn_, preferred_element_type=F32)
    b = lax.dot_general(y, wb_ref[...].astype(BF16), tn_, preferred_element_type=F32)
    branch = a * jax.nn.sigmoid(b)
    o_ref[0] = (at_ref[0].astype(F32) + gs_ref[0].astype(F32) * branch).astype(o_ref.dtype)


def _glu_merge(y_t, w_glu, attn_phase, gs_phase, *, tn=512):
    _, width, nc = y_t.shape
    d_model = attn_phase.shape[-1]
    nj = d_model // tn
    blk = pl.BlockSpec((1, nc, tn), lambda j, t: (t, 0, j))
    return pl.pallas_call(
        _glu_kernel,
        grid=(nj, CHUNK),
        in_specs=[
            pl.BlockSpec((1, width, nc), lambda j, t: (t, 0, 0)),
            pl.BlockSpec((width, tn), lambda j, t: (0, j)),
            pl.BlockSpec((width, tn), lambda j, t: (0, nj + j)),
            blk, blk,
        ],
        out_specs=blk,
        out_shape=jax.ShapeDtypeStruct(attn_phase.shape, BF16),
        compiler_params=_cparams("arbitrary", "arbitrary"),
        name="glu",
    )(y_t, w_glu, w_glu, attn_phase, gs_phase)


def _outp_kernel(a_ref, w_ref, x_ref, g_ref, o_ref, nat_ref):
    grp = CHUNK * CHUNK

    @pl.when(pl.program_id(1) == 0)
    def _():
        perm = _phase_perm()
        for q in range(nat_ref.shape[0] // grp):
            a = a_ref[:, q * CHUNK:(q + 1) * CHUNK, :].reshape(grp, a_ref.shape[-1])
            nat_ref[q * grp:(q + 1) * grp, :] = jnp.dot(perm, a, preferred_element_type=F32).astype(nat_ref.dtype)

    acc = jnp.dot(nat_ref[...], w_ref[...].astype(BF16), preferred_element_type=F32)
    o_ref[...] = x_ref[...] + g_ref[...] * acc


def _out_proj(a_phase, w, x, mod, gate_idx, *, tm=1024, tn=512):
    n_ph, nc, k = a_phase.shape
    s, n = x.shape
    tm = min(tm, s)
    return pl.pallas_call(
        _outp_kernel,
        grid=(s // tm, n // tn),
        in_specs=[
            pl.BlockSpec((n_ph, tm // n_ph, k), lambda i, j: (0, i, 0)),
            pl.BlockSpec((k, tn), lambda i, j: (0, j)),
            pl.BlockSpec((tm, tn), lambda i, j: (i, j)),
            pl.BlockSpec((1, tn), lambda i, j: (0, gate_idx * (n // tn) + j)),
        ],
        out_specs=pl.BlockSpec((tm, tn), lambda i, j: (i, j)),
        out_shape=jax.ShapeDtypeStruct((s, n), F32),
        scratch_shapes=[pltpu.VMEM((tm, k), BF16)],
        compiler_params=_cparams("arbitrary", "arbitrary"),
        name="outp",
    )(a_phase, w, x, mod)


def _ff1_kernel(h_ref, w_ref, o_ref):
    acc = jnp.dot(h_ref[...], w_ref[...].astype(BF16), preferred_element_type=F32)
    r = jnp.maximum(acc, 0.0)
    o_ref[...] = (r * r).astype(o_ref.dtype)


def _ff1(h, w, *, tm=1024, tn=512):
    s, k = h.shape
    n = w.shape[1]
    tm = min(tm, s)
    return pl.pallas_call(
        _ff1_kernel,
        grid=(s // tm, n // tn),
        in_specs=[pl.BlockSpec((tm, k), lambda i, j: (i, 0)), pl.BlockSpec((k, tn), lambda i, j: (0, j))],
        out_specs=pl.BlockSpec((tm, tn), lambda i, j: (i, j)),
        out_shape=jax.ShapeDtypeStruct((s, n), BF16),
        compiler_params=_cparams("arbitrary", "arbitrary"),
        name="ff1",
    )(h, w)


def _ff2_kernel(a_ref, w_ref, x_ref, g_ref, o_ref, acc_ref):
    kk = pl.program_id(2)

    @pl.when(kk == 0)
    def _():
        acc_ref[...] = jnp.zeros_like(acc_ref)

    acc_ref[...] += jnp.dot(a_ref[...], w_ref[...].astype(BF16), preferred_element_type=F32)

    @pl.when(kk == pl.num_programs(2) - 1)
    def _():
        o_ref[...] = x_ref[...] + g_ref[...] * acc_ref[...]


def _ff2(a, w, x, mod, gate_idx, *, tm=1024, tn=1024, tk=2048):
    s, k = a.shape
    n = w.shape[1]
    tm = min(tm, s)
    return pl.pallas_call(
        _ff2_kernel,
        grid=(s // tm, n // tn, k // tk),
        in_specs=[
            pl.BlockSpec((tm, tk), lambda i, j, kk: (i, kk)),
            pl.BlockSpec((tk, tn), lambda i, j, kk: (kk, j)),
            pl.BlockSpec((tm, tn), lambda i, j, kk: (i, j)),
            pl.BlockSpec((1, tn), lambda i, j, kk: (0, gate_idx * (n // tn) + j)),
        ],
        out_specs=pl.BlockSpec((tm, tn), lambda i, j, kk: (i, j)),
        out_shape=jax.ShapeDtypeStruct((s, n), F32),
        scratch_shapes=[pltpu.VMEM((tm, tn), F32)],
        compiler_params=_cparams("arbitrary", "arbitrary", "arbitrary"),
        name="ff2",
    )(a, w, x, mod)


def _layer(x, c_col, pos_col, w_ada, b_ada, norm1_g, norm2_g, w_in, q_norm_g, k_norm_g, attn_sinks,
           lam_re, lam_im, log_dt, b_re, b_im, c_re, c_im, ssm_d, w_glu, w_out, w_ff1, w_ff2):
    s, d = x.shape
    kv_width = N_KV_HEADS * HEAD_DIM
    ssm_width = ssm_d.shape[0]
    n_groups = ssm_width // SSM_GROUP
    nc = s // CHUNK
    n_levels = max(1, (nc - 1).bit_length())
    u_col0 = d + 2 * kv_width

    mod = _modulation(c_col, w_ada, b_ada.reshape(1, -1))
    h, h_phase = _norm_mod(x, norm1_g.reshape(1, d), mod, 0, with_phase_major=True)
    rope_c, rope_sa, rope_sb = _rope_tables(pos_col)
    pz = _proj(h, w_in, q_norm_g.reshape(1, -1), k_norm_g.reshape(1, -1), rope_c, rope_sa, rope_sb,
               d_model=d, kv_width=kv_width, ssm_width=ssm_width)
    u_t, gs_phase = _proj_u(h_phase, w_in, u_col0=u_col0, gs_col0=u_col0 + ssm_width + d, ssm_width=ssm_width,
                            d_model=d)

    m_op, w_op, v_op, a_pow = _s5_prep(log_dt, lam_re, lam_im, b_re, b_im, c_re, c_im, n_levels=n_levels)
    d_col = jnp.tile(ssm_d.reshape(n_groups, 1, SSM_GROUP), (1, CHUNK, 1)).reshape(n_groups, CHUNK * SSM_GROUP, 1)
    y_t = _s5_scan(u_t, m_op, w_op, v_op, a_pow, d_col, n_levels=n_levels)

    attn_phase = _attention(attn_sinks, pz, d_model=d, kv_width=kv_width)
    merged_phase = _glu_merge(y_t, w_glu, attn_phase, gs_phase)
    x1 = _out_proj(merged_phase, w_out, x, mod, 2)

    (h2,) = _norm_mod(x1, norm2_g.reshape(1, d), mod, 3)
    act = _ff1(h2, w_ff1)
    return _ff2(act, w_ff2, x1, mod, 5)


def kernel(x, c, positions, w_ada, b_ada, norm1_g, norm2_g, w_in, q_norm_g, k_norm_g, attn_sinks, ssm_lam_re,
           ssm_lam_im, ssm_log_dt, ssm_b_re, ssm_b_im, ssm_c_re, ssm_c_im, ssm_d, w_glu, w_out, w_ff1, w_ff2):
    bsz, s, d = x.shape
    assert bsz == 1 and d == N_Q_HEADS * HEAD_DIM and s % (CHUNK * LANES) == 0
    xs = x[0]
    c_col = c.reshape(d, 1)
    pos_col = positions.reshape(s, 1)
    for l in range(w_ada.shape[0]):
        xs = _layer(xs, c_col, pos_col, w_ada[l], b_ada[l], norm1_g[l], norm2_g[l], w_in[l], q_norm_g[l],
                    k_norm_g[l], attn_sinks[l], ssm_lam_re[l], ssm_lam_im[l], ssm_log_dt[l], ssm_b_re[l],
                    ssm_b_im[l], ssm_c_re[l], ssm_c_im[l], ssm_d[l], w_glu[l], w_out[l], w_ff1[l], w_ff2[l])
    return xs[None]
```

```python
import functools
import math

import jax
import jax.numpy as jnp
from jax import lax
from jax.experimental import pallas as pl
from jax.experimental.pallas import tpu as pltpu

F32 = jnp.float32
BF16 = jnp.bfloat16

LANES = 128
SUBLANES = 8
MXU_DIM = 256
VMEM_LIMIT_BYTES = 56 * 1024 * 1024

N_Q_HEADS = 32
N_KV_HEADS = 8
HEAD_DIM = 128
REP = N_Q_HEADS // N_KV_HEADS
WINDOW = 128
ROPE_DIM = HEAD_DIM // 4
ROPE_HALF = ROPE_DIM // 2
ROPE_THETA = 500000.0
SSM_GROUP = 16
SSM_STATE = 64
N_MOD = 6
EPS = 1e-6
MASK_VALUE = -1e30
CHUNK = MXU_DIM // SSM_GROUP


def _phase_perm():
    n = CHUNK * CHUNK
    r = lax.broadcasted_iota(jnp.int32, (n, n), 0)
    c = lax.broadcasted_iota(jnp.int32, (n, n), 1)
    return jnp.where((r % CHUNK) * CHUNK + r // CHUNK == c, 1.0, 0.0).astype(BF16)


def _to_phase_major(v, perm):
    return jnp.dot(perm, v, preferred_element_type=F32).astype(v.dtype).reshape(CHUNK, CHUNK, v.shape[-1])


def _cparams(*sem):
    return pltpu.CompilerParams(dimension_semantics=sem, vmem_limit_bytes=VMEM_LIMIT_BYTES)


def _mod_kernel(c_ref, w_ref, b_ref, o_ref, *, rk):
    d, tn = w_ref.shape

    def body(r, acc):
        rows = pl.ds(pl.multiple_of(r * rk, rk), rk)
        c = c_ref[rows, :]
        ca = c * jax.nn.sigmoid(c)
        prod = w_ref[rows, :] * ca
        return acc + jnp.sum(prod.reshape(rk // SUBLANES, SUBLANES, tn), axis=0)

    acc = lax.fori_loop(0, d // rk, body, jnp.zeros((SUBLANES, tn), F32))
    o_ref[...] = jnp.sum(acc, axis=0, keepdims=True) + b_ref[...]


def _modulation(c_col, w_ada, b_ada_row, *, tn=512, rk=512):
    d, n = w_ada.shape
    return pl.pallas_call(
        functools.partial(_mod_kernel, rk=rk),
        grid=(n // tn,),
        in_specs=[
            pl.BlockSpec((d, 1), lambda j: (0, 0)),
            pl.BlockSpec((d, tn), lambda j: (0, j)),
            pl.BlockSpec((1, tn), lambda j: (0, j)),
        ],
        out_specs=pl.BlockSpec((1, tn), lambda j: (0, j)),
        out_shape=jax.ShapeDtypeStruct((1, n), F32),
        compiler_params=_cparams("arbitrary"),
        name="mod",
    )(c_col, w_ada, b_ada_row)


def _norm_kernel(x_ref, g_ref, sh_ref, sc_ref, o_ref, *phase_ref):
    def normed(x):
        ms = jnp.mean(x * x, axis=-1, keepdims=True)
        y = x * lax.rsqrt(ms + EPS) * g_ref[...]
        return (y * (1.0 + sc_ref[...]) + sh_ref[...]).astype(o_ref.dtype)

    hb = normed(x_ref[...])
    o_ref[...] = hb
    for oph_ref in phase_ref:
        perm = _phase_perm()
        grp = CHUNK * CHUNK
        for q in range(hb.shape[0] // grp):
            oph_ref[:, q * CHUNK:(q + 1) * CHUNK, :] = _to_phase_major(hb[q * grp:(q + 1) * grp], perm)


def _norm_mod(x, g_row, mod, shift_idx, *, with_phase_major=False, tm=256):
    s, d = x.shape
    out_specs = [pl.BlockSpec((tm, d), lambda i: (i, 0))]
    out_shape = [jax.ShapeDtypeStruct((s, d), BF16)]
    if with_phase_major:
        out_specs.append(pl.BlockSpec((CHUNK, tm // CHUNK, d), lambda i: (0, i, 0)))
        out_shape.append(jax.ShapeDtypeStruct((CHUNK, s // CHUNK, d), BF16))
    return pl.pallas_call(
        _norm_kernel,
        grid=(s // tm,),
        in_specs=[
            pl.BlockSpec((tm, d), lambda i: (i, 0)),
            pl.BlockSpec((1, d), lambda i: (0, 0)),
            pl.BlockSpec((1, d), lambda i: (0, shift_idx)),
            pl.BlockSpec((1, d), lambda i: (0, shift_idx + 1)),
        ],
        out_specs=out_specs,
        out_shape=out_shape,
        compiler_params=_cparams("arbitrary"),
        name="norm",
    )(x, g_row, mod, mod)


def _rope_kernel(pos_ref, c_ref, sa_ref, sb_ref):
    pos = pos_ref[...].astype(F32)
    lane = lax.broadcasted_iota(jnp.int32, (1, LANES), 1)
    idx = (lane & (ROPE_HALF - 1)).astype(F32)
    inv_freq = jnp.power(jnp.float32(ROPE_THETA), -2.0 * idx / ROPE_DIM)
    ang = pos * inv_freq
    cos = jnp.cos(ang)
    sin = jnp.sin(ang)
    c_ref[...] = jnp.where(lane < ROPE_DIM, cos, 1.0)
    sa_ref[...] = jnp.where(lane < ROPE_HALF, -sin, 0.0)
    sb_ref[...] = jnp.where((lane >= ROPE_HALF) & (lane < ROPE_DIM), sin, 0.0)


def _rope_tables(pos_col, *, tm=1024):
    s = pos_col.shape[0]
    tm = min(tm, s)
    spec = pl.BlockSpec((tm, LANES), lambda i: (i, 0))
    shp = jax.ShapeDtypeStruct((s, LANES), F32)
    return pl.pallas_call(
        _rope_kernel,
        grid=(s // tm,),
        in_specs=[pl.BlockSpec((tm, 1), lambda i: (i, 0))],
        out_specs=[spec, spec, spec],
        out_shape=[shp, shp, shp],
        compiler_params=_cparams("arbitrary"),
        name="rope",
    )(pos_col)


def _proj_kernel(h_ref, w_ref, qg_ref, kg_ref, rc_ref, rsa_ref, rsb_ref, o_ref, acc0_ref, acc1_ref, *,
                 n_j, n_q, n_qk, n_qkv):
    s = pl.program_id(0)
    jp = (s - 1) % n_j
    tn = o_ref.shape[1]

    def qk_epilogue(src):
        g = jnp.where(jp < n_q, qg_ref[...], kg_ref[...])
        c, sa, sb = rc_ref[...], rsa_ref[...], rsb_ref[...]
        for hh in range(tn // HEAD_DIM):
            cols = slice(hh * HEAD_DIM, (hh + 1) * HEAD_DIM)
            xh = src[:, cols]
            ms = jnp.mean(xh * xh, axis=-1, keepdims=True)
            y = xh * lax.rsqrt(ms + EPS) * g
            y = y * c + pltpu.roll(y, HEAD_DIM - ROPE_HALF, 1) * sa + pltpu.roll(y, ROPE_HALF, 1) * sb
            o_ref[:, cols] = y.astype(o_ref.dtype)

    def v_epilogue(src):
        o_ref[...] = src[...].astype(o_ref.dtype)

    def gate_epilogue(src):
        o_ref[...] = jax.nn.sigmoid(src[...]).astype(o_ref.dtype)

    cases = [
        (s == 0, None),
        ((s > 0) & (jp < n_qk), qk_epilogue),
        ((s > 0) & (jp >= n_qk) & (jp < n_qkv), v_epilogue),
        ((s > 0) & (jp >= n_qkv), gate_epilogue),
    ]
    for parity, (dst, src) in enumerate(((acc0_ref, acc1_ref), (acc1_ref, acc0_ref))):
        for cond, epilogue in cases:
            @pl.when(cond & (s % 2 == parity))
            def _(dst=dst, src=src, epilogue=epilogue):
                if epilogue is not None:
                    epilogue(src)
                dst[...] = jnp.dot(h_ref[...], w_ref[...].astype(BF16), preferred_element_type=F32)


def _proj(h, w_in, qg_row, kg_row, rope_c, rope_sa, rope_sb, *, d_model, kv_width, ssm_width, tm=1024, tn=512):
    s, d = h.shape
    tm = min(tm, s)
    n_q = d_model // tn
    n_qk = n_q + kv_width // tn
    n_qkv = n_qk + kv_width // tn
    n_u = ssm_width // tn
    n_j = n_qkv + d_model // tn
    n_tiles = (s // tm) * n_j

    def cur(t):
        return jnp.minimum(t, n_tiles - 1)

    def prev(t):
        return jnp.maximum(t - 1, 0)

    def w_map(t):
        j = cur(t) % n_j
        return 0, jnp.where(j < n_qkv, j, j + n_u)

    tab = pl.BlockSpec((tm, LANES), lambda t: (prev(t) // n_j, 0))
    row = pl.BlockSpec((1, HEAD_DIM), lambda t: (0, 0))
    return pl.pallas_call(
        functools.partial(_proj_kernel, n_j=n_j, n_q=n_q, n_qk=n_qk, n_qkv=n_qkv),
        grid=(n_tiles + 1,),
        in_specs=[
            pl.BlockSpec((tm, d), lambda t: (cur(t) // n_j, 0)),
            pl.BlockSpec((d, tn), w_map),
            row, row, tab, tab, tab,
        ],
        out_specs=pl.BlockSpec((tm, tn), lambda t: (prev(t) // n_j, prev(t) % n_j)),
        out_shape=jax.ShapeDtypeStruct((s, n_j * tn), BF16),
        scratch_shapes=[pltpu.VMEM((tm, tn), F32), pltpu.VMEM((tm, tn), F32)],
        compiler_params=_cparams("arbitrary"),
        name="proj",
    )(h, w_in, qg_row, kg_row, rope_c, rope_sa, rope_sb)


def _proju_kernel(h_ref, w_ref, ut_ref, gs_ref, *, n_u):
    j = pl.program_id(1)
    wb = w_ref[...].astype(BF16)

    @pl.when(j < n_u)
    def _():
        for p in range(h_ref.shape[0]):
            ut_ref[p] = jnp.dot(h_ref[p], wb, preferred_element_type=F32).T

    @pl.when(j >= n_u)
    def _():
        for p in range(h_ref.shape[0]):
            gs_ref[p] = jax.nn.sigmoid(jnp.dot(h_ref[p], wb, preferred_element_type=F32)).astype(gs_ref.dtype)


def _proj_u(h_phase, w_in, *, u_col0, gs_col0, ssm_width, d_model, phases=2, tn=512):
    _, nc, d = h_phase.shape
    n_u = ssm_width // tn
    n_gs = d_model // tn
    return pl.pallas_call(
        functools.partial(_proju_kernel, n_u=n_u),
        grid=(CHUNK // phases, n_u + n_gs),
        in_specs=[
            pl.BlockSpec((phases, nc, d), lambda t, j: (t, 0, 0)),
            pl.BlockSpec((d, tn), lambda t, j: (0, jnp.where(j < n_u, u_col0 // tn + j, gs_col0 // tn + j - n_u))),
        ],
        out_specs=[
            pl.BlockSpec((phases, tn, nc), lambda t, j: (t, jnp.minimum(j, n_u - 1), 0)),
            pl.BlockSpec((phases, nc, tn), lambda t, j: (t, 0, jnp.maximum(j - n_u, 0))),
        ],
        out_shape=[jax.ShapeDtypeStruct((CHUNK, ssm_width, nc), F32),
                   jax.ShapeDtypeStruct((CHUNK, nc, d_model), BF16)],
        compiler_params=_cparams("arbitrary", "arbitrary"),
        name="proj_u",
    )(h_phase, w_in)


def _cmul(ar, ai, br, bi):
    return ar * br - ai * bi, ar * bi + ai * br


def _s5prep_kernel(ldt_ref, lr_ref, li_ref, bre_ref, bim_ref, cre_ref, cim_ref, m_ref, w_ref, v_ref, ap_ref, *,
                   n_levels):
    for gi in range(m_ref.shape[0]):
        _s5prep_group(gi, ldt_ref, lr_ref, li_ref, bre_ref, bim_ref, cre_ref, cim_ref, m_ref, w_ref, v_ref, ap_ref,
                      n_levels)


def _s5prep_group(gi, ldt_ref, lr_ref, li_ref, bre_ref, bim_ref, cre_ref, cim_ref, m_ref, w_ref, v_ref, ap_ref,
                  n_levels):
    p_ = SSM_STATE
    kdim = CHUNK * SSM_GROUP
    first = lax.broadcasted_iota(jnp.int32, (1, 2 * p_), 1) < p_
    dt = jnp.exp(ldt_ref[gi])
    lr, li = lr_ref[gi], li_ref[gi]
    mag = jnp.exp(lr * dt)
    are, aim = mag * jnp.cos(li * dt), mag * jnp.sin(li * dt)
    den = lr * lr + li * li
    nr = are - 1.0
    coef_re = (nr * lr + aim * li) / den
    coef_im = (aim * lr - nr * li) / den
    bt_re, bt_im = bre_ref[gi], bim_ref[gi]
    bbar_re = coef_re * bt_re - coef_im * bt_im
    bbar_im = coef_re * bt_im + coef_im * bt_re

    pow_re = [jnp.ones_like(are)]
    pow_im = [jnp.zeros_like(are)]
    for _ in range(CHUNK):
        nre, nim = _cmul(pow_re[-1], pow_im[-1], are, aim)
        pow_re.append(nre)
        pow_im.append(nim)

    blocks = []
    for m in range(CHUNK):
        g_re, g_im = _cmul(pow_re[CHUNK - 1 - m], pow_im[CHUNK - 1 - m], bbar_re, bbar_im)
        blocks.append(jnp.where(first, g_re, g_im))
    g_t = jnp.concatenate(blocks, axis=0)
    w_ref[gi] = g_t.T.astype(w_ref.dtype)

    cre2, cim2 = cre_ref[gi], cim_ref[gi]
    c_mix = jnp.where(first, cre2, -cim2)
    k_rev = lax.dot_general(c_mix, g_t, (((1,), (1,)), ((), ())), precision=lax.Precision.HIGHEST,
                            preferred_element_type=F32)
    k_ext = jnp.concatenate([k_rev, jnp.zeros_like(k_rev)], axis=1)
    for t in range(CHUNK):
        off = (CHUNK - 1 - t) * SSM_GROUP
        m_ref[gi, t * SSM_GROUP:(t + 1) * SSM_GROUP, :] = k_ext[:, off:off + kdim].astype(m_ref.dtype)

    for t in range(CHUNK):
        pr, pi = pow_re[t + 1], pow_im[t + 1]
        blk_v = jnp.where(first, cre2 * pr - cim2 * pi, -(cre2 * pi + cim2 * pr))
        v_ref[gi, t * SSM_GROUP:(t + 1) * SSM_GROUP, :] = blk_v.astype(v_ref.dtype)

    sub = lax.broadcasted_iota(jnp.int32, (2 * p_, 1), 0)
    qr, qi = pow_re[CHUNK], pow_im[CHUNK]
    levels = jnp.zeros((2 * p_, 2 * p_), F32)
    for k in range(n_levels):
        levels = jnp.where(sub == k, qr, levels)
        levels = jnp.where(sub == p_ + k, qi, levels)
        qr, qi = _cmul(qr, qi, qr, qi)
    ap_ref[gi] = levels.T[:p_]


def _s5_prep(log_dt, lam_re, lam_im, b_re, b_im, c_re, c_im, *, n_levels, gb=4):
    g, p_ = lam_re.shape

    def dup(a):
        return jnp.concatenate([a, a], axis=-1)

    ldt = log_dt.reshape(g, 1, 1)
    lr, li = dup(lam_re).reshape(g, 1, 2 * p_), dup(lam_im).reshape(g, 1, 2 * p_)
    bt_re, bt_im = dup(jnp.swapaxes(b_re, 1, 2)), dup(jnp.swapaxes(b_im, 1, 2))
    kdim = CHUNK * SSM_GROUP

    def spec(*shape):
        return pl.BlockSpec((gb,) + shape, lambda i: (i, 0, 0))

    return pl.pallas_call(
        functools.partial(_s5prep_kernel, n_levels=n_levels),
        grid=(g // gb,),
        in_specs=[spec(1, 1), spec(1, 2 * p_), spec(1, 2 * p_), spec(SSM_GROUP, 2 * p_), spec(SSM_GROUP, 2 * p_),
                  spec(SSM_GROUP, 2 * p_), spec(SSM_GROUP, 2 * p_)],
        out_specs=[spec(kdim, kdim), spec(2 * p_, kdim), spec(kdim, 2 * p_), spec(p_, 2 * p_)],
        out_shape=[jax.ShapeDtypeStruct((g, kdim, kdim), BF16),
                   jax.ShapeDtypeStruct((g, 2 * p_, kdim), BF16),
                   jax.ShapeDtypeStruct((g, kdim, 2 * p_), BF16),
                   jax.ShapeDtypeStruct((g, p_, 2 * p_), F32)],
        compiler_params=_cparams("arbitrary"),
        name="s5prep",
    )(ldt, lr, li, bt_re, bt_im, dup(c_re), dup(c_im))


def _s5_kernel(x_ref, m_ref, w_ref, v_ref, ap_ref, d_ref, o_ref, *, n_levels):
    for gi in range(m_ref.shape[0]):
        _s5_group(gi, x_ref, m_ref, w_ref, v_ref, ap_ref, d_ref, o_ref, n_levels)


def _s5_group(gi, x_ref, m_ref, w_ref, v_ref, ap_ref, d_ref, o_ref, n_levels):
    p_ = SSM_STATE
    kdim = CHUNK * SSM_GROUP
    nc = x_ref.shape[-1]
    chans = slice(gi * SSM_GROUP, (gi + 1) * SSM_GROUP)
    x = x_ref[:, chans, :].reshape(kdim, nc)
    xb = x.astype(BF16)
    y = jnp.dot(m_ref[gi], xb, preferred_element_type=F32)
    loc = jnp.dot(w_ref[gi], xb, preferred_element_type=F32)
    s_re, s_im = loc[:p_], loc[p_:]
    lane = lax.broadcasted_iota(jnp.int32, (1, nc), 1)
    ap = ap_ref[gi]
    for k in range(n_levels):
        sh = 1 << k
        pr, pi = ap[:, k:k + 1], ap[:, p_ + k:p_ + k + 1]
        keep = lane >= sh
        t_re = jnp.where(keep, pltpu.roll(s_re, sh, 1), 0.0)
        t_im = jnp.where(keep, pltpu.roll(s_im, sh, 1), 0.0)
        s_re, s_im = s_re + (pr * t_re - pi * t_im), s_im + (pr * t_im + pi * t_re)
    keep = lane >= 1
    start = jnp.concatenate([jnp.where(keep, pltpu.roll(s_re, 1, 1), 0.0),
                             jnp.where(keep, pltpu.roll(s_im, 1, 1), 0.0)], axis=0)
    y = y + jnp.dot(v_ref[gi], start.astype(BF16), preferred_element_type=F32)
    y = y + d_ref[gi] * x
    o_ref[:, chans, :] = jax.nn.gelu(y).astype(o_ref.dtype).reshape(CHUNK, SSM_GROUP, nc)


def _s5_scan(u_t, m_op, w_op, v_op, a_pow, d_col, *, n_levels, gb=2):
    _, width, nc = u_t.shape
    g = width // SSM_GROUP
    kdim = CHUNK * SSM_GROUP
    p2 = 2 * SSM_STATE

    def spec(*shape):
        return pl.BlockSpec((gb,) + shape, lambda i: (i, 0, 0))

    io = pl.BlockSpec((CHUNK, gb * SSM_GROUP, nc), lambda i: (0, i, 0))
    return pl.pallas_call(
        functools.partial(_s5_kernel, n_levels=n_levels),
        grid=(g // gb,),
        in_specs=[io, spec(kdim, kdim), spec(p2, kdim), spec(kdim, p2), spec(SSM_STATE, p2), spec(kdim, 1)],
        out_specs=io,
        out_shape=jax.ShapeDtypeStruct((CHUNK, width, nc), BF16),
        compiler_params=_cparams("arbitrary"),
        name="s5",
    )(u_t, m_op, w_op, v_op, a_pow, d_col)


def _attn_kernel(sink_ref, q_ref, kc_ref, kp_ref, vc_ref, vp_ref, ga_ref, o_ref, nat_ref, *, n_groups):
    i = pl.program_id(0)
    gh = pl.program_id(1)
    tq = q_ref.shape[0]
    nb = tq // WINDOW
    scale = HEAD_DIM ** -0.5
    row = lax.broadcasted_iota(jnp.int32, (WINDOW, WINDOW), 0)
    col = lax.broadcasted_iota(jnp.int32, (WINDOW, WINDOW), 1)
    tri1 = col <= row
    tri = jnp.concatenate([tri1] * REP, axis=0)
    nt = (((1,), (1,)), ((), ()))
    for b in range(nb):
        rows = slice(b * WINDOW, (b + 1) * WINDOW)
        has_prev = (i * nb + b) > 0
        valid = tri | has_prev
        for g in range(n_groups):
            kcols = slice(g * HEAD_DIM, (g + 1) * HEAD_DIM)
            if b == 0:
                k_prev, v_prev = kp_ref[:, kcols], vp_ref[:, kcols]
            else:
                prows = slice((b - 1) * WINDOW, b * WINDOW)
                k_prev, v_prev = kc_ref[prows, kcols], vc_ref[prows, kcols]
            k_band = jnp.concatenate([k_prev, kc_ref[rows, kcols]], axis=0)
            v_band = jnp.concatenate([v_prev, vc_ref[rows, kcols]], axis=0)
            qcols = [slice((g * REP + r) * HEAD_DIM, (g * REP + r + 1) * HEAD_DIM) for r in range(REP)]
            qg = jnp.concatenate([q_ref[rows, c] for c in qcols], axis=0)
            s2 = lax.dot_general(qg, k_band, nt, preferred_element_type=F32)
            s = jnp.where(tri, s2[:, WINDOW:], s2[:, :WINDOW]) * scale
            s = jnp.where(valid, s, MASK_VALUE)
            probs = []
            for r in range(REP):
                sink = sink_ref[(gh * n_groups + g) * REP + r]
                sr = s[r * WINDOW:(r + 1) * WINDOW]
                m = jnp.maximum(jnp.max(sr, axis=-1, keepdims=True), sink)
                p = jnp.exp(sr - m)
                denom = jnp.sum(p, axis=-1, keepdims=True) + jnp.exp(sink - m)
                probs.append(p * (1.0 / denom))
            pr = jnp.concatenate(probs, axis=0)
            p2 = jnp.concatenate([jnp.where(tri, 0.0, pr), jnp.where(tri, pr, 0.0)], axis=1).astype(BF16)
            o = jnp.dot(p2, v_band, preferred_element_type=F32)
            for r in range(REP):
                gate = ga_ref[rows, qcols[r]].astype(F32)
                nat_ref[rows, qcols[r]] = gate * o[r * WINDOW:(r + 1) * WINDOW]
    perm = _phase_perm()
    grp = CHUNK * CHUNK
    for q in range(tq // grp):
        nat = nat_ref[q * grp:(q + 1) * grp, :].astype(o_ref.dtype)
        o_ref[:, q * CHUNK:(q + 1) * CHUNK, :] = _to_phase_major(nat, perm)


def _attention(sinks, pz, *, d_model, kv_width, tq=256, n_groups=4):
    s = pz.shape[0]
    tq = min(tq, s)
    nb = tq // WINDOW
    wq = n_groups * REP * HEAD_DIM
    wk = n_groups * HEAD_DIM
    k0 = d_model // wk
    v0 = (d_model + kv_width) // wk
    a0 = (d_model + 2 * kv_width) // wq

    def prev_map(col0):
        return lambda i, gh: (jnp.maximum(i * nb - 1, 0), col0 + gh)

    return pl.pallas_call(
        functools.partial(_attn_kernel, n_groups=n_groups),
        grid=(s // tq, d_model // wq),
        in_specs=[
            pl.BlockSpec(memory_space=pltpu.SMEM),
            pl.BlockSpec((tq, wq), lambda i, gh: (i, gh)),
            pl.BlockSpec((tq, wk), lambda i, gh: (i, k0 + gh)),
            pl.BlockSpec((WINDOW, wk), prev_map(k0)),
            pl.BlockSpec((tq, wk), lambda i, gh: (i, v0 + gh)),
            pl.BlockSpec((WINDOW, wk), prev_map(v0)),
            pl.BlockSpec((tq, wq), lambda i, gh: (i, a0 + gh)),
        ],
        out_specs=pl.BlockSpec((CHUNK, tq // CHUNK, wq), lambda i, gh: (0, i, gh)),
        out_shape=jax.ShapeDtypeStruct((CHUNK, s // CHUNK, d_model), BF16),
        scratch_shapes=[pltpu.VMEM((tq, wq), F32)],
        compiler_params=_cparams("arbitrary", "arbitrary"),
        name="attn",
    )(sinks, pz, pz, pz, pz, pz, pz)


def _glu_kernel(y_ref, wa_ref, wb_ref, at_ref, gs_ref, o_ref):
    tn_ = (((0,), (0,)), ((), ()))
    y = y_ref[0]
    a = lax.dot_general(y, wa_ref[...].astype(BF16), tn_, preferred_element_type=F32)
    b = lax.dot_general(y, wb_ref[...].astype(BF16), tn_, preferred_element_type=F32)
    branch = a * jax.nn.sigmoid(b)
    o_ref[0] = (at_ref[0].astype(F32) + gs_ref[0].astype(F32) * branch).astype(o_ref.dtype)


def _glu_merge(y_t, w_glu, attn_phase, gs_phase, *, tn=512):
    _, width, nc = y_t.shape
    d_model = attn_phase.shape[-1]
    nj = d_model // tn
    blk = pl.BlockSpec((1, nc, tn), lambda j, t: (t, 0, j))
    return pl.pallas_call(
        _glu_kernel,
        grid=(nj, CHUNK),
        in_specs=[
            pl.BlockSpec((1, width, nc), lambda j, t: (t, 0, 0)),
            pl.BlockSpec((width, tn), lambda j, t: (0, j)),
            pl.BlockSpec((width, tn), lambda j, t: (0, nj + j)),
            blk, blk,
        ],
        out_specs=blk,
        out_shape=jax.ShapeDtypeStruct(attn_phase.shape, BF16),
        compiler_params=_cparams("arbitrary", "arbitrary"),
        name="glu",
    )(y_t, w_glu, w_glu, attn_phase, gs_phase)


def _outp_kernel(a_ref, w_ref, x_ref, g_ref, o_ref, nat_ref):
    grp = CHUNK * CHUNK

    @pl.when(pl.program_id(1) == 0)
    def _():
        perm = _phase_perm()
        for q in range(nat_ref.shape[0] // grp):
            a = a_ref[:, q * CHUNK:(q + 1) * CHUNK, :].reshape(grp, a_ref.shape[-1])
            nat_ref[q * grp:(q + 1) * grp, :] = jnp.dot(perm, a, preferred_element_type=F32).astype(nat_ref.dtype)

    acc = jnp.dot(nat_ref[...], w_ref[...].astype(BF16), preferred_element_type=F32)
    o_ref[...] = x_ref[...] + g_ref[...] * acc


def _out_proj(a_phase, w, x, mod, gate_idx, *, tm=1024, tn=512):
    n_ph, nc, k = a_phase.shape
    s, n = x.shape
    tm = min(tm, s)
    return pl.pallas_call(
        _outp_kernel,
        grid=(s // tm, n // tn),
        in_specs=[
            pl.BlockSpec((n_ph, tm // n_ph, k), lambda i, j: (0, i, 0)),
            pl.BlockSpec((k, tn), lambda i, j: (0, j)),
            pl.BlockSpec((tm, tn), lambda i, j: (i, j)),
            pl.BlockSpec((1, tn), lambda i, j: (0, gate_idx * (n // tn) + j)),
        ],
        out_specs=pl.BlockSpec((tm, tn), lambda i, j: (i, j)),
        out_shape=jax.ShapeDtypeStruct((s, n), F32),
        scratch_shapes=[pltpu.VMEM((tm, k), BF16)],
        compiler_params=_cparams("arbitrary", "arbitrary"),
        name="outp",
    )(a_phase, w, x, mod)


def _ff1_kernel(h_ref, w_ref, o_ref):
    acc = jnp.dot(h_ref[...], w_ref[...].astype(BF16), preferred_element_type=F32)
    r = jnp.maximum(acc, 0.0)
    o_ref[...] = (r * r).astype(o_ref.dtype)


def _ff1(h, w, *, tm=1024, tn=512):
    s, k = h.shape
    n = w.shape[1]
    tm = min(tm, s)
    return pl.pallas_call(
        _ff1_kernel,
        grid=(s // tm, n // tn),
        in_specs=[pl.BlockSpec((tm, k), lambda i, j: (i, 0)), pl.BlockSpec((k, tn), lambda i, j: (0, j))],
        out_specs=pl.BlockSpec((tm, tn), lambda i, j: (i, j)),
        out_shape=jax.ShapeDtypeStruct((s, n), BF16),
        compiler_params=_cparams("arbitrary", "arbitrary"),
        name="ff1",
    )(h, w)


def _ff2_kernel(a_ref, w_ref, x_ref, g_ref, o_ref, acc_ref):
    kk = pl.program_id(2)

    @pl.when(kk == 0)
    def _():
        acc_ref[...] = jnp.zeros_like(acc_ref)

    acc_ref[...] += jnp.dot(a_ref[...], w_ref[...].astype(BF16), preferred_element_type=F32)

    @pl.when(kk == pl.num_programs(2) - 1)
    def _():
        o_ref[...] = x_ref[...] + g_ref[...] * acc_ref[...]


def _ff2(a, w, x, mod, gate_idx, *, tm=1024, tn=1024, tk=2048):
    s, k = a.shape
    n = w.shape[1]
    tm = min(tm, s)
    return pl.pallas_call(
        _ff2_kernel,
        grid=(s // tm, n // tn, k // tk),
        in_specs=[
            pl.BlockSpec((tm, tk), lambda i, j, kk: (i, kk)),
            pl.BlockSpec((tk, tn), lambda i, j, kk: (kk, j)),
            pl.BlockSpec((tm, tn), lambda i, j, kk: (i, j)),
            pl.BlockSpec((1, tn), lambda i, j, kk: (0, gate_idx * (n // tn) + j)),
        ],
        out_specs=pl.BlockSpec((tm, tn), lambda i, j, kk: (i, j)),
        out_shape=jax.ShapeDtypeStruct((s, n), F32),
        scratch_shapes=[pltpu.VMEM((tm, tn), F32)],
        compiler_params=_cparams("arbitrary", "arbitrary", "arbitrary"),
        name="ff2",
    )(a, w, x, mod)


def _layer(x, c_col, pos_col, w_ada, b_ada, norm1_g, norm2_g, w_in, q_norm_g, k_norm_g, attn_sinks,
           lam_re, lam_im, log_dt, b_re, b_im, c_re, c_im, ssm_d, w_glu, w_out, w_ff1, w_ff2):
    s, d = x.shape
    kv_width = N_KV_HEADS * HEAD_DIM
    ssm_width = ssm_d.shape[0]
    n_groups = ssm_width // SSM_GROUP
    nc = s // CHUNK
    n_levels = max(1, (nc - 1).bit_length())
    u_col0 = d + 2 * kv_width

    mod = _modulation(c_col, w_ada, b_ada.reshape(1, -1))
    h, h_phase = _norm_mod(x, norm1_g.reshape(1, d), mod, 0, with_phase_major=True)
    rope_c, rope_sa, rope_sb = _rope_tables(pos_col)
    pz = _proj(h, w_in, q_norm_g.reshape(1, -1), k_norm_g.reshape(1, -1), rope_c, rope_sa, rope_sb,
               d_model=d, kv_width=kv_width, ssm_width=ssm_width)
    u_t, gs_phase = _proj_u(h_phase, w_in, u_col0=u_col0, gs_col0=u_col0 + ssm_width + d, ssm_width=ssm_width,
                            d_model=d)

    m_op, w_op, v_op, a_pow = _s5_prep(log_dt, lam_re, lam_im, b_re, b_im, c_re, c_im, n_levels=n_levels)
    d_col = jnp.tile(ssm_d.reshape(n_groups, 1, SSM_GROUP), (1, CHUNK, 1)).reshape(n_groups, CHUNK * SSM_GROUP, 1)
    y_t = _s5_scan(u_t, m_op, w_op, v_op, a_pow, d_col, n_levels=n_levels)

    attn_phase = _attention(attn_sinks, pz, d_model=d, kv_width=kv_width)
    merged_phase = _glu_merge(y_t, w_glu, attn_phase, gs_phase)
    x1 = _out_proj(merged_phase, w_out, x, mod, 2)

    (h2,) = _norm_mod(x1, norm2_g.reshape(1, d), mod, 3)
    act = _ff1(h2, w_ff1)
    return _ff2(act, w_ff2, x1, mod, 5)


def kernel(x, c, positions, w_ada, b_ada, norm1_g, norm2_g, w_in, q_norm_g, k_norm_g, attn_sinks, ssm_lam_re,
           ssm_lam_im, ssm_log_dt, ssm_b_re, ssm_b_im, ssm_c_re, ssm_c_im, ssm_d, w_glu, w_out, w_ff1, w_ff2):
    bsz, s, d = x.shape
    assert bsz == 1 and d == N_Q_HEADS * HEAD_DIM and s % (CHUNK * LANES) == 0
    xs = x[0]
    c_col = c.reshape(d, 1)
    pos_col = positions.reshape(s, 1)
    for l in range(w_ada.shape[0]):
        xs = _layer(xs, c_col, pos_col, w_ada[l], b_ada[l], norm1_g[l], norm2_g[l], w_in[l], q_norm_g[l],
                    k_norm_g[l], attn_sinks[l], ssm_lam_re[l], ssm_lam_im[l], ssm_log_dt[l], ssm_b_re[l],
                    ssm_b_im[l], ssm_c_re[l], ssm_c_im[l], ssm_d[l], w_glu[l], w_out[l], w_ff1[l], w_ff2[l])
    return xs[None]
```

```python
import functools
import math

import jax
import jax.numpy as jnp
from jax import lax
from jax.experimental import pallas as pl
from jax.experimental.pallas import tpu as pltpu

F32 = jnp.float32
BF16 = jnp.bfloat16

LANES = 128
SUBLANES = 8
MXU_DIM = 256
VMEM_LIMIT_BYTES = 56 * 1024 * 1024

N_Q_HEADS = 32
N_KV_HEADS = 8
HEAD_DIM = 128
REP = N_Q_HEADS // N_KV_HEADS
WINDOW = 128
ROPE_DIM = HEAD_DIM // 4
ROPE_HALF = ROPE_DIM // 2
ROPE_THETA = 500000.0
SSM_GROUP = 16
SSM_STATE = 64
N_MOD = 6
EPS = 1e-6
MASK_VALUE = -1e30
CHUNK = MXU_DIM // SSM_GROUP


def _phase_perm():
    n = CHUNK * CHUNK
    r = lax.broadcasted_iota(jnp.int32, (n, n), 0)
    c = lax.broadcasted_iota(jnp.int32, (n, n), 1)
    return jnp.where((r % CHUNK) * CHUNK + r // CHUNK == c, 1.0, 0.0).astype(BF16)


def _to_phase_major(v, perm):
    return jnp.dot(perm, v, preferred_element_type=F32).astype(v.dtype).reshape(CHUNK, CHUNK, v.shape[-1])


def _lagged_tiles(n_tiles, n_inner):
    def cur(step):
        t = jnp.minimum(step, n_tiles - 1)
        return t // n_inner, t % n_inner

    def prev(step):
        t = jnp.maximum(step - 1, 0)
        return t // n_inner, t % n_inner

    return cur, prev


def _overlapped_step(step, run_matmul, epilogues, accs):
    cases = [(step == 0, None)] + [((step > 0) & cond, fn) for cond, fn in epilogues]
    for parity in (0, 1):
        for cond, fn in cases:
            @pl.when(cond & (step % 2 == parity))
            def _(dst=accs[parity], src=accs[1 - parity], fn=fn):
                if fn is not None:
                    fn(src)
                run_matmul(dst)


def _cparams(*sem):
    return pltpu.CompilerParams(dimension_semantics=sem, vmem_limit_bytes=VMEM_LIMIT_BYTES)


def _mod_kernel(c_ref, w_ref, b_ref, o_ref, *, rk):
    d, tn = w_ref.shape

    def body(r, acc):
        rows = pl.ds(pl.multiple_of(r * rk, rk), rk)
        c = c_ref[rows, :]
        ca = c * jax.nn.sigmoid(c)
        prod = w_ref[rows, :] * ca
        return acc + jnp.sum(prod.reshape(rk // SUBLANES, SUBLANES, tn), axis=0)

    acc = lax.fori_loop(0, d // rk, body, jnp.zeros((SUBLANES, tn), F32))
    o_ref[...] = jnp.sum(acc, axis=0, keepdims=True) + b_ref[...]


def _modulation(c_col, w_ada, b_ada_row, *, tn=512, rk=512):
    d, n = w_ada.shape
    return pl.pallas_call(
        functools.partial(_mod_kernel, rk=rk),
        grid=(n // tn,),
        in_specs=[
            pl.BlockSpec((d, 1), lambda j: (0, 0)),
            pl.BlockSpec((d, tn), lambda j: (0, j)),
            pl.BlockSpec((1, tn), lambda j: (0, j)),
        ],
        out_specs=pl.BlockSpec((1, tn), lambda j: (0, j)),
        out_shape=jax.ShapeDtypeStruct((1, n), F32),
        compiler_params=_cparams("arbitrary"),
        name="mod",
    )(c_col, w_ada, b_ada_row)


def _norm_kernel(x_ref, g_ref, sh_ref, sc_ref, o_ref, *phase_ref):
    def normed(x):
        ms = jnp.mean(x * x, axis=-1, keepdims=True)
        y = x * lax.rsqrt(ms + EPS) * g_ref[...]
        return (y * (1.0 + sc_ref[...]) + sh_ref[...]).astype(o_ref.dtype)

    hb = normed(x_ref[...])
    o_ref[...] = hb
    for oph_ref in phase_ref:
        perm = _phase_perm()
        grp = CHUNK * CHUNK
        for q in range(hb.shape[0] // grp):
            oph_ref[:, q * CHUNK:(q + 1) * CHUNK, :] = _to_phase_major(hb[q * grp:(q + 1) * grp], perm)


def _norm_mod(x, g_row, mod, shift_idx, *, with_phase_major=False, tm=512):
    s, d = x.shape
    out_specs = [pl.BlockSpec((tm, d), lambda i: (i, 0))]
    out_shape = [jax.ShapeDtypeStruct((s, d), BF16)]
    if with_phase_major:
        out_specs.append(pl.BlockSpec((CHUNK, tm // CHUNK, d), lambda i: (0, i, 0)))
        out_shape.append(jax.ShapeDtypeStruct((CHUNK, s // CHUNK, d), BF16))
    return pl.pallas_call(
        _norm_kernel,
        grid=(s // tm,),
        in_specs=[
            pl.BlockSpec((tm, d), lambda i: (i, 0)),
            pl.BlockSpec((1, d), lambda i: (0, 0)),
            pl.BlockSpec((1, d), lambda i: (0, shift_idx)),
            pl.BlockSpec((1, d), lambda i: (0, shift_idx + 1)),
        ],
        out_specs=out_specs,
        out_shape=out_shape,
        compiler_params=_cparams("arbitrary"),
        name="norm",
    )(x, g_row, mod, mod)


def _rope_kernel(pos_ref, c_ref, sa_ref, sb_ref):
    pos = pos_ref[...].astype(F32)
    lane = lax.broadcasted_iota(jnp.int32, (1, LANES), 1)
    idx = (lane & (ROPE_HALF - 1)).astype(F32)
    inv_freq = jnp.power(jnp.float32(ROPE_THETA), -2.0 * idx / ROPE_DIM)
    ang = pos * inv_freq
    cos = jnp.cos(ang)
    sin = jnp.sin(ang)
    c_ref[...] = jnp.where(lane < ROPE_DIM, cos, 1.0)
    sa_ref[...] = jnp.where(lane < ROPE_HALF, -sin, 0.0)
    sb_ref[...] = jnp.where((lane >= ROPE_HALF) & (lane < ROPE_DIM), sin, 0.0)


def _rope_tables(pos_col, *, tm=1024):
    s = pos_col.shape[0]
    tm = min(tm, s)
    spec = pl.BlockSpec((tm, LANES), lambda i: (i, 0))
    shp = jax.ShapeDtypeStruct((s, LANES), F32)
    return pl.pallas_call(
        _rope_kernel,
        grid=(s // tm,),
        in_specs=[pl.BlockSpec((tm, 1), lambda i: (i, 0))],
        out_specs=[spec, spec, spec],
        out_shape=[shp, shp, shp],
        compiler_params=_cparams("arbitrary"),
        name="rope",
    )(pos_col)


def _proj_kernel(h_ref, w_ref, qg_ref, kg_ref, rc_ref, rsa_ref, rsb_ref, o_ref, acc0_ref, acc1_ref, *,
                 n_j, n_q, n_qk, n_qkv):
    step = pl.program_id(0)
    jp = (step - 1) % n_j
    tn = o_ref.shape[1]

    def qk_epilogue(src):
        g = jnp.where(jp < n_q, qg_ref[...], kg_ref[...])
        c, sa, sb = rc_ref[...], rsa_ref[...], rsb_ref[...]
        for hh in range(tn // HEAD_DIM):
            cols = slice(hh * HEAD_DIM, (hh + 1) * HEAD_DIM)
            xh = src[:, cols]
            ms = jnp.mean(xh * xh, axis=-1, keepdims=True)
            y = xh * lax.rsqrt(ms + EPS) * g
            y = y * c + pltpu.roll(y, HEAD_DIM - ROPE_HALF, 1) * sa + pltpu.roll(y, ROPE_HALF, 1) * sb
            o_ref[:, cols] = y.astype(o_ref.dtype)

    def v_epilogue(src):
        o_ref[...] = src[...].astype(o_ref.dtype)

    def gate_epilogue(src):
        o_ref[...] = jax.nn.sigmoid(src[...]).astype(o_ref.dtype)

    def run_matmul(dst):
        dst[...] = jnp.dot(h_ref[...], w_ref[...].astype(BF16), preferred_element_type=F32)

    _overlapped_step(step, run_matmul,
                     [(jp < n_qk, qk_epilogue), ((jp >= n_qk) & (jp < n_qkv), v_epilogue), (jp >= n_qkv, gate_epilogue)],
                     (acc0_ref, acc1_ref))


def _proj(h, w_in, qg_row, kg_row, rope_c, rope_sa, rope_sb, *, d_model, kv_width, ssm_width, tm=1024, tn=512):
    s, d = h.shape
    tm = min(tm, s)
    n_q = d_model // tn
    n_qk = n_q + kv_width // tn
    n_qkv = n_qk + kv_width // tn
    n_u = ssm_width // tn
    n_j = n_qkv + d_model // tn
    n_tiles = (s // tm) * n_j
    cur, prev = _lagged_tiles(n_tiles, n_j)

    def w_map(t):
        j = cur(t)[1]
        return 0, jnp.where(j < n_qkv, j, j + n_u)

    tab = pl.BlockSpec((tm, LANES), lambda t: (prev(t)[0], 0))
    row = pl.BlockSpec((1, HEAD_DIM), lambda t: (0, 0))
    return pl.pallas_call(
        functools.partial(_proj_kernel, n_j=n_j, n_q=n_q, n_qk=n_qk, n_qkv=n_qkv),
        grid=(n_tiles + 1,),
        in_specs=[
            pl.BlockSpec((tm, d), lambda t: (cur(t)[0], 0)),
            pl.BlockSpec((d, tn), w_map),
            row, row, tab, tab, tab,
        ],
        out_specs=pl.BlockSpec((tm, tn), lambda t: prev(t)),
        out_shape=jax.ShapeDtypeStruct((s, n_j * tn), BF16),
        scratch_shapes=[pltpu.VMEM((tm, tn), F32), pltpu.VMEM((tm, tn), F32)],
        compiler_params=_cparams("arbitrary"),
        name="proj",
    )(h, w_in, qg_row, kg_row, rope_c, rope_sa, rope_sb)


def _proju_kernel(h_ref, w_ref, ut_ref, gs_ref, acc0_ref, acc1_ref, *, n_j, n_u):
    step = pl.program_id(0)
    jp = (step - 1) % n_j
    n_ph, nc, d = h_ref.shape

    def u_epilogue(src):
        for p in range(n_ph):
            ut_ref[p] = src[p * nc:(p + 1) * nc, :].T

    def gate_epilogue(src):
        for p in range(n_ph):
            gs_ref[p] = jax.nn.sigmoid(src[p * nc:(p + 1) * nc, :]).astype(gs_ref.dtype)

    def run_matmul(dst):
        dst[...] = jnp.dot(h_ref[...].reshape(n_ph * nc, d), w_ref[...].astype(BF16), preferred_element_type=F32)

    _overlapped_step(step, run_matmul, [(jp < n_u, u_epilogue), (jp >= n_u, gate_epilogue)], (acc0_ref, acc1_ref))


def _proj_u(h_phase, w_in, *, u_col0, gs_col0, ssm_width, d_model, phases=2, tn=512):
    _, nc, d = h_phase.shape
    n_u = ssm_width // tn
    n_j = n_u + d_model // tn
    n_tiles = (CHUNK // phases) * n_j
    cur, prev = _lagged_tiles(n_tiles, n_j)

    def w_map(s):
        j = cur(s)[1]
        return 0, jnp.where(j < n_u, u_col0 // tn + j, gs_col0 // tn + j - n_u)

    def ut_map(s):
        t, j = prev(s)
        return t, jnp.minimum(j, n_u - 1), 0

    def gs_map(s):
        t, j = prev(s)
        return t, 0, jnp.maximum(j - n_u, 0)

    return pl.pallas_call(
        functools.partial(_proju_kernel, n_j=n_j, n_u=n_u),
        grid=(n_tiles + 1,),
        in_specs=[
            pl.BlockSpec((phases, nc, d), lambda s: (cur(s)[0], 0, 0)),
            pl.BlockSpec((d, tn), w_map),
        ],
        out_specs=[
            pl.BlockSpec((phases, tn, nc), ut_map),
            pl.BlockSpec((phases, nc, tn), gs_map),
        ],
        out_shape=[jax.ShapeDtypeStruct((CHUNK, ssm_width, nc), F32),
                   jax.ShapeDtypeStruct((CHUNK, nc, d_model), BF16)],
        scratch_shapes=[pltpu.VMEM((phases * nc, tn), F32), pltpu.VMEM((phases * nc, tn), F32)],
        compiler_params=_cparams("arbitrary"),
        name="proj_u",
    )(h_phase, w_in)


def _cmul(ar, ai, br, bi):
    return ar * br - ai * bi, ar * bi + ai * br


def _s5prep_kernel(ldt_ref, lr_ref, li_ref, bre_ref, bim_ref, cre_ref, cim_ref, m_ref, w_ref, v_ref, ap_ref, *,
                   n_levels):
    for gi in range(m_ref.shape[0]):
        _s5prep_group(gi, ldt_ref, lr_ref, li_ref, bre_ref, bim_ref, cre_ref, cim_ref, m_ref, w_ref, v_ref, ap_ref,
                      n_levels)


def _s5prep_group(gi, ldt_ref, lr_ref, li_ref, bre_ref, bim_ref, cre_ref, cim_ref, m_ref, w_ref, v_ref, ap_ref,
                  n_levels):
    p_ = SSM_STATE
    kdim = CHUNK * SSM_GROUP
    first = lax.broadcasted_iota(jnp.int32, (1, 2 * p_), 1) < p_
    dt = jnp.exp(ldt_ref[gi])
    lr, li = lr_ref[gi], li_ref[gi]
    mag = jnp.exp(lr * dt)
    are, aim = mag * jnp.cos(li * dt), mag * jnp.sin(li * dt)
    den = lr * lr + li * li
    nr = are - 1.0
    coef_re = (nr * lr + aim * li) / den
    coef_im = (aim * lr - nr * li) / den
    bt_re, bt_im = bre_ref[gi], bim_ref[gi]
    bbar_re = coef_re * bt_re - coef_im * bt_im
    bbar_im = coef_re * bt_im + coef_im * bt_re

    pow_re = [jnp.ones_like(are)]
    pow_im = [jnp.zeros_like(are)]
    for _ in range(CHUNK):
        nre, nim = _cmul(pow_re[-1], pow_im[-1], are, aim)
        pow_re.append(nre)
        pow_im.append(nim)

    blocks = []
    for m in range(CHUNK):
        g_re, g_im = _cmul(pow_re[CHUNK - 1 - m], pow_im[CHUNK - 1 - m], bbar_re, bbar_im)
        blocks.append(jnp.where(first, g_re, g_im))
    g_t = jnp.concatenate(blocks, axis=0)
    w_ref[gi] = g_t.T.astype(w_ref.dtype)

    cre2, cim2 = cre_ref[gi], cim_ref[gi]
    c_mix = jnp.where(first, cre2, -cim2)
    k_rev = lax.dot_general(c_mix, g_t, (((1,), (1,)), ((), ())), precision=lax.Precision.HIGHEST,
                            preferred_element_type=F32)
    k_ext = jnp.concatenate([k_rev, jnp.zeros_like(k_rev)], axis=1)
    for t in range(CHUNK):
        off = (CHUNK - 1 - t) * SSM_GROUP
        m_ref[gi, t * SSM_GROUP:(t + 1) * SSM_GROUP, :] = k_ext[:, off:off + kdim].astype(m_ref.dtype)

    for t in range(CHUNK):
        pr, pi = pow_re[t + 1], pow_im[t + 1]
        blk_v = jnp.where(first, cre2 * pr - cim2 * pi, -(cre2 * pi + cim2 * pr))
        v_ref[gi, t * SSM_GROUP:(t + 1) * SSM_GROUP, :] = blk_v.astype(v_ref.dtype)

    sub = lax.broadcasted_iota(jnp.int32, (2 * p_, 1), 0)
    qr, qi = pow_re[CHUNK], pow_im[CHUNK]
    levels = jnp.zeros((2 * p_, 2 * p_), F32)
    for k in range(n_levels):
        levels = jnp.where(sub == k, qr, levels)
        levels = jnp.where(sub == p_ + k, qi, levels)
        qr, qi = _cmul(qr, qi, qr, qi)
    ap_ref[gi] = levels.T[:p_]


def _s5_prep(log_dt, lam_re, lam_im, b_re, b_im, c_re, c_im, *, n_levels, gb=4):
    g, p_ = lam_re.shape

    def dup(a):
        return jnp.concatenate([a, a], axis=-1)

    ldt = log_dt.reshape(g, 1, 1)
    lr, li = dup(lam_re).reshape(g, 1, 2 * p_), dup(lam_im).reshape(g, 1, 2 * p_)
    bt_re, bt_im = dup(jnp.swapaxes(b_re, 1, 2)), dup(jnp.swapaxes(b_im, 1, 2))
    kdim = CHUNK * SSM_GROUP

    def spec(*shape):
        return pl.BlockSpec((gb,) + shape, lambda i: (i, 0, 0))

    return pl.pallas_call(
        functools.partial(_s5prep_kernel, n_levels=n_levels),
        grid=(g // gb,),
        in_specs=[spec(1, 1), spec(1, 2 * p_), spec(1, 2 * p_), spec(SSM_GROUP, 2 * p_), spec(SSM_GROUP, 2 * p_),
                  spec(SSM_GROUP, 2 * p_), spec(SSM_GROUP, 2 * p_)],
        out_specs=[spec(kdim, kdim), spec(2 * p_, kdim), spec(kdim, 2 * p_), spec(p_, 2 * p_)],
        out_shape=[jax.ShapeDtypeStruct((g, kdim, kdim), BF16),
                   jax.ShapeDtypeStruct((g, 2 * p_, kdim), BF16),
                   jax.ShapeDtypeStruct((g, kdim, 2 * p_), BF16),
                   jax.ShapeDtypeStruct((g, p_, 2 * p_), F32)],
        compiler_params=_cparams("arbitrary"),
        name="s5prep",
    )(ldt, lr, li, bt_re, bt_im, dup(c_re), dup(c_im))


def _s5_kernel(x_ref, m_ref, w_ref, v_ref, ap_ref, d_ref, o_ref, *, n_levels):
    for gi in range(m_ref.shape[0]):
        _s5_group(gi, x_ref, m_ref, w_ref, v_ref, ap_ref, d_ref, o_ref, n_levels)


def _s5_group(gi, x_ref, m_ref, w_ref, v_ref, ap_ref, d_ref, o_ref, n_levels):
    p_ = SSM_STATE
    kdim = CHUNK * SSM_GROUP
    nc = x_ref.shape[-1]
    chans = slice(gi * SSM_GROUP, (gi + 1) * SSM_GROUP)
    x = x_ref[:, chans, :].reshape(kdim, nc)
    xb = x.astype(BF16)
    y = jnp.dot(m_ref[gi], xb, preferred_element_type=F32)
    loc = jnp.dot(w_ref[gi], xb, preferred_element_type=F32)
    s_re, s_im = loc[:p_], loc[p_:]
    lane = lax.broadcasted_iota(jnp.int32, (1, nc), 1)
    ap = ap_ref[gi]
    for k in range(n_levels):
        sh = 1 << k
        pr, pi = ap[:, k:k + 1], ap[:, p_ + k:p_ + k + 1]
        keep = lane >= sh
        t_re = jnp.where(keep, pltpu.roll(s_re, sh, 1), 0.0)
        t_im = jnp.where(keep, pltpu.roll(s_im, sh, 1), 0.0)
        s_re, s_im = s_re + (pr * t_re - pi * t_im), s_im + (pr * t_im + pi * t_re)
    keep = lane >= 1
    start = jnp.concatenate([jnp.where(keep, pltpu.roll(s_re, 1, 1), 0.0),
                             jnp.where(keep, pltpu.roll(s_im, 1, 1), 0.0)], axis=0)
    y = y + jnp.dot(v_ref[gi], start.astype(BF16), preferred_element_type=F32)
    y = y + d_ref[gi] * x
    o_ref[:, chans, :] = jax.nn.gelu(y).astype(o_ref.dtype).reshape(CHUNK, SSM_GROUP, nc)


def _s5_scan(u_t, m_op, w_op, v_op, a_pow, d_col, *, n_levels, gb=2):
    _, width, nc = u_t.shape
    g = width // SSM_GROUP
    kdim = CHUNK * SSM_GROUP
    p2 = 2 * SSM_STATE

    def spec(*shape):
        return pl.BlockSpec((gb,) + shape, lambda i: (i, 0, 0))

    io = pl.BlockSpec((CHUNK, gb * SSM_GROUP, nc), lambda i: (0, i, 0))
    return pl.pallas_call(
        functools.partial(_s5_kernel, n_levels=n_levels),
        grid=(g // gb,),
        in_specs=[io, spec(kdim, kdim), spec(p2, kdim), spec(kdim, p2), spec(SSM_STATE, p2), spec(kdim, 1)],
        out_specs=io,
        out_shape=jax.ShapeDtypeStruct((CHUNK, width, nc), BF16),
        compiler_params=_cparams("arbitrary"),
        name="s5",
    )(u_t, m_op, w_op, v_op, a_pow, d_col)


def _attn_kernel(sink_ref, q_ref, kc_ref, kp_ref, vc_ref, vp_ref, ga_ref, o_ref, nat_ref, *, n_groups):
    i = pl.program_id(0)
    gh = pl.program_id(1)
    tq = q_ref.shape[0]
    nb = tq // WINDOW
    scale = HEAD_DIM ** -0.5
    row = lax.broadcasted_iota(jnp.int32, (WINDOW, WINDOW), 0)
    col = lax.broadcasted_iota(jnp.int32, (WINDOW, WINDOW), 1)
    tri1 = col <= row
    tri = jnp.concatenate([tri1] * REP, axis=0)
    nt = (((1,), (1,)), ((), ()))
    for b in range(nb):
        rows = slice(b * WINDOW, (b + 1) * WINDOW)
        has_prev = (i * nb + b) > 0
        valid = tri | has_prev
        for g in range(n_groups):
            kcols = slice(g * HEAD_DIM, (g + 1) * HEAD_DIM)
            if b == 0:
                k_prev, v_prev = kp_ref[:, kcols], vp_ref[:, kcols]
            else:
                prows = slice((b - 1) * WINDOW, b * WINDOW)
                k_prev, v_prev = kc_ref[prows, kcols], vc_ref[prows, kcols]
            k_band = jnp.concatenate([k_prev, kc_ref[rows, kcols]], axis=0)
            v_band = jnp.concatenate([v_prev, vc_ref[rows, kcols]], axis=0)
            qcols = [slice((g * REP + r) * HEAD_DIM, (g * REP + r + 1) * HEAD_DIM) for r in range(REP)]
            qg = jnp.concatenate([q_ref[rows, c] for c in qcols], axis=0)
            s2 = lax.dot_general(qg, k_band, nt, preferred_element_type=F32)
            s = jnp.where(tri, s2[:, WINDOW:], s2[:, :WINDOW]) * scale
            s = jnp.where(valid, s, MASK_VALUE)
            probs = []
            for r in range(REP):
                sink = sink_ref[(gh * n_groups + g) * REP + r]
                sr = s[r * WINDOW:(r + 1) * WINDOW]
                m = jnp.maximum(jnp.max(sr, axis=-1, keepdims=True), sink)
                p = jnp.exp(sr - m)
                denom = jnp.sum(p, axis=-1, keepdims=True) + jnp.exp(sink - m)
                probs.append(p * (1.0 / denom))
            pr = jnp.concatenate(probs, axis=0)
            p2 = jnp.concatenate([jnp.where(tri, 0.0, pr), jnp.where(tri, pr, 0.0)], axis=1).astype(BF16)
            o = jnp.dot(p2, v_band, preferred_element_type=F32)
            for r in range(REP):
                gate = ga_ref[rows, qcols[r]].astype(F32)
                nat_ref[rows, qcols[r]] = gate * o[r * WINDOW:(r + 1) * WINDOW]
    perm = _phase_perm()
    grp = CHUNK * CHUNK
    for q in range(tq // grp):
        nat = nat_ref[q * grp:(q + 1) * grp, :].astype(o_ref.dtype)
        o_ref[:, q * CHUNK:(q + 1) * CHUNK, :] = _to_phase_major(nat, perm)


def _attention(sinks, pz, *, d_model, kv_width, tq=256, n_groups=4):
    s = pz.shape[0]
    tq = min(tq, s)
    nb = tq // WINDOW
    wq = n_groups * REP * HEAD_DIM
    wk = n_groups * HEAD_DIM
    k0 = d_model // wk
    v0 = (d_model + kv_width) // wk
    a0 = (d_model + 2 * kv_width) // wq

    def prev_map(col0):
        return lambda i, gh: (jnp.maximum(i * nb - 1, 0), col0 + gh)

    return pl.pallas_call(
        functools.partial(_attn_kernel, n_groups=n_groups),
        grid=(s // tq, d_model // wq),
        in_specs=[
            pl.BlockSpec(memory_space=pltpu.SMEM),
            pl.BlockSpec((tq, wq), lambda i, gh: (i, gh)),
            pl.BlockSpec((tq, wk), lambda i, gh: (i, k0 + gh)),
            pl.BlockSpec((WINDOW, wk), prev_map(k0)),
            pl.BlockSpec((tq, wk), lambda i, gh: (i, v0 + gh)),
            pl.BlockSpec((WINDOW, wk), prev_map(v0)),
            pl.BlockSpec((tq, wq), lambda i, gh: (i, a0 + gh)),
        ],
        out_specs=pl.BlockSpec((CHUNK, tq // CHUNK, wq), lambda i, gh: (0, i, gh)),
        out_shape=jax.ShapeDtypeStruct((CHUNK, s // CHUNK, d_model), BF16),
        scratch_shapes=[pltpu.VMEM((tq, wq), F32)],
        compiler_params=_cparams("arbitrary", "arbitrary"),
        name="attn",
    )(sinks, pz, pz, pz, pz, pz, pz)


def _glu_kernel(y_ref, wa_ref, wb_ref, at_ref, gs_ref, o_ref, yt_ref):
    n_ph, _, nc = y_ref.shape

    @pl.when(pl.program_id(1) == 0)
    def _():
        for p in range(n_ph):
            yt_ref[p * nc:(p + 1) * nc, :] = y_ref[p].T

    y = yt_ref[...]
    a = jnp.dot(y, wa_ref[...].astype(BF16), preferred_element_type=F32)
    b = jnp.dot(y, wb_ref[...].astype(BF16), preferred_element_type=F32)
    branch = a * jax.nn.sigmoid(b)
    for p in range(n_ph):
        rows = slice(p * nc, (p + 1) * nc)
        o_ref[p] = (at_ref[p].astype(F32) + gs_ref[p].astype(F32) * branch[rows]).astype(o_ref.dtype)


def _glu_merge(y_t, w_glu, attn_phase, gs_phase, *, phases=2, tn=512):
    _, width, nc = y_t.shape
    d_model = attn_phase.shape[-1]
    nj = d_model // tn
    blk = pl.BlockSpec((phases, nc, tn), lambda t, j: (t, 0, j))
    return pl.pallas_call(
        _glu_kernel,
        grid=(CHUNK // phases, nj),
        in_specs=[
            pl.BlockSpec((phases, width, nc), lambda t, j: (t, 0, 0)),
            pl.BlockSpec((width, tn), lambda t, j: (0, j)),
            pl.BlockSpec((width, tn), lambda t, j: (0, nj + j)),
            blk, blk,
        ],
        out_specs=blk,
        out_shape=jax.ShapeDtypeStruct(attn_phase.shape, BF16),
        scratch_shapes=[pltpu.VMEM((phases * nc, width), BF16)],
        compiler_params=_cparams("arbitrary", "arbitrary"),
        name="glu",
    )(y_t, w_glu, w_glu, attn_phase, gs_phase)


def _outp_kernel(a_ref, w_ref, x_ref, g_ref, o_ref, nat_ref):
    grp = CHUNK * CHUNK

    @pl.when(pl.program_id(1) == 0)
    def _():
        perm = _phase_perm()
        for q in range(nat_ref.shape[0] // grp):
            a = a_ref[:, q * CHUNK:(q + 1) * CHUNK, :].reshape(grp, a_ref.shape[-1])
            nat_ref[q * grp:(q + 1) * grp, :] = jnp.dot(perm, a, preferred_element_type=F32).astype(nat_ref.dtype)

    acc = jnp.dot(nat_ref[...], w_ref[...].astype(BF16), preferred_element_type=F32)
    o_ref[...] = x_ref[...] + g_ref[...] * acc


def _out_proj(a_phase, w, x, mod, gate_idx, *, tm=1024, tn=512):
    n_ph, nc, k = a_phase.shape
    s, n = x.shape
    tm = min(tm, s)
    return pl.pallas_call(
        _outp_kernel,
        grid=(s // tm, n // tn),
        in_specs=[
            pl.BlockSpec((n_ph, tm // n_ph, k), lambda i, j: (0, i, 0)),
            pl.BlockSpec((k, tn), lambda i, j: (0, j)),
            pl.BlockSpec((tm, tn), lambda i, j: (i, j)),
            pl.BlockSpec((1, tn), lambda i, j: (0, gate_idx * (n // tn) + j)),
        ],
        out_specs=pl.BlockSpec((tm, tn), lambda i, j: (i, j)),
        out_shape=jax.ShapeDtypeStruct((s, n), F32),
        scratch_shapes=[pltpu.VMEM((tm, k), BF16)],
        compiler_params=_cparams("arbitrary", "arbitrary"),
        name="outp",
    )(a_phase, w, x, mod)


def _ff1_kernel(h_ref, w_ref, o_ref):
    acc = jnp.dot(h_ref[...], w_ref[...].astype(BF16), preferred_element_type=F32)
    r = jnp.maximum(acc, 0.0)
    o_ref[...] = (r * r).astype(o_ref.dtype)


def _ff1(h, w, *, tm=1024, tn=512):
    s, k = h.shape
    n = w.shape[1]
    tm = min(tm, s)
    return pl.pallas_call(
        _ff1_kernel,
        grid=(s // tm, n // tn),
        in_specs=[pl.BlockSpec((tm, k), lambda i, j: (i, 0)), pl.BlockSpec((k, tn), lambda i, j: (0, j))],
        out_specs=pl.BlockSpec((tm, tn), lambda i, j: (i, j)),
        out_shape=jax.ShapeDtypeStruct((s, n), BF16),
        compiler_params=_cparams("arbitrary", "arbitrary"),
        name="ff1",
    )(h, w)


def _ff2_kernel(a_ref, w_ref, x_ref, g_ref, o_ref):
    kk = pl.program_id(2)
    last = pl.num_programs(2) - 1

    def product():
        return jnp.dot(a_ref[...], w_ref[...].astype(BF16), preferred_element_type=F32)

    @pl.when(kk == 0)
    def _():
        o_ref[...] = product()

    @pl.when((kk > 0) & (kk < last))
    def _():
        o_ref[...] += product()

    @pl.when(kk == last)
    def _():
        o_ref[...] = x_ref[...] + g_ref[...] * (o_ref[...] + product())


def _ff2(a, w, x, mod, gate_idx, *, tm=2048, tn=1024, tk=1024):
    s, k = a.shape
    n = w.shape[1]
    tm = min(tm, s)
    assert k // tk >= 2
    return pl.pallas_call(
        _ff2_kernel,
        grid=(s // tm, n // tn, k // tk),
        in_specs=[
            pl.BlockSpec((tm, tk), lambda i, j, kk: (i, kk)),
            pl.BlockSpec((tk, tn), lambda i, j, kk: (kk, j)),
            pl.BlockSpec((tm, tn), lambda i, j, kk: (i, j)),
            pl.BlockSpec((1, tn), lambda i, j, kk: (0, gate_idx * (n // tn) + j)),
        ],
        out_specs=pl.BlockSpec((tm, tn), lambda i, j, kk: (i, j)),
        out_shape=jax.ShapeDtypeStruct((s, n), F32),
        compiler_params=_cparams("arbitrary", "arbitrary", "arbitrary"),
        name="ff2",
    )(a, w, x, mod)


def _layer(x, c_col, pos_col, w_ada, b_ada, norm1_g, norm2_g, w_in, q_norm_g, k_norm_g, attn_sinks,
           lam_re, lam_im, log_dt, b_re, b_im, c_re, c_im, ssm_d, w_glu, w_out, w_ff1, w_ff2):
    s, d = x.shape
    kv_width = N_KV_HEADS * HEAD_DIM
    ssm_width = ssm_d.shape[0]
    n_groups = ssm_width // SSM_GROUP
    nc = s // CHUNK
    n_levels = max(1, (nc - 1).bit_length())
    u_col0 = d + 2 * kv_width

    mod = _modulation(c_col, w_ada, b_ada.reshape(1, -1))
    h, h_phase = _norm_mod(x, norm1_g.reshape(1, d), mod, 0, with_phase_major=True)
    rope_c, rope_sa, rope_sb = _rope_tables(pos_col)
    pz = _proj(h, w_in, q_norm_g.reshape(1, -1), k_norm_g.reshape(1, -1), rope_c, rope_sa, rope_sb,
               d_model=d, kv_width=kv_width, ssm_width=ssm_width)
    u_t, gs_phase = _proj_u(h_phase, w_in, u_col0=u_col0, gs_col0=u_col0 + ssm_width + d, ssm_width=ssm_width,
                            d_model=d)

    m_op, w_op, v_op, a_pow = _s5_prep(log_dt, lam_re, lam_im, b_re, b_im, c_re, c_im, n_levels=n_levels)
    d_col = jnp.tile(ssm_d.reshape(n_groups, 1, SSM_GROUP), (1, CHUNK, 1)).reshape(n_groups, CHUNK * SSM_GROUP, 1)
    y_t = _s5_scan(u_t, m_op, w_op, v_op, a_pow, d_col, n_levels=n_levels)

    attn_phase = _attention(attn_sinks, pz, d_model=d, kv_width=kv_width)
    merged_phase = _glu_merge(y_t, w_glu, attn_phase, gs_phase)
    x1 = _out_proj(merged_phase, w_out, x, mod, 2)

    (h2,) = _norm_mod(x1, norm2_g.reshape(1, d), mod, 3)
    act = _ff1(h2, w_ff1)
    return _ff2(act, w_ff2, x1, mod, 5)


def kernel(x, c, positions, w_ada, b_ada, norm1_g, norm2_g, w_in, q_norm_g, k_norm_g, attn_sinks, ssm_lam_re,
           ssm_lam_im, ssm_log_dt, ssm_b_re, ssm_b_im, ssm_c_re, ssm_c_im, ssm_d, w_glu, w_out, w_ff1, w_ff2):
    bsz, s, d = x.shape
    assert bsz == 1 and d == N_Q_HEADS * HEAD_DIM and s % (CHUNK * LANES) == 0
    xs = x[0]
    c_col = c.reshape(d, 1)
    pos_col = positions.reshape(s, 1)
    for l in range(w_ada.shape[0]):
        xs = _layer(xs, c_col, pos_col, w_ada[l], b_ada[l], norm1_g[l], norm2_g[l], w_in[l], q_norm_g[l],
                    k_norm_g[l], attn_sinks[l], ssm_lam_re[l], ssm_lam_im[l], ssm_log_dt[l], ssm_b_re[l],
                    ssm_b_im[l], ssm_c_re[l], ssm_c_im[l], ssm_d[l], w_glu[l], w_out[l], w_ff1[l], w_ff2[l])
    return xs[None]
```

```python
import functools
import math

import jax
import jax.numpy as jnp
from jax import lax
from jax.experimental import pallas as pl
from jax.experimental.pallas import tpu as pltpu

F32 = jnp.float32
BF16 = jnp.bfloat16

LANES = 128
SUBLANES = 8
MXU_DIM = 256
VMEM_LIMIT_BYTES = 56 * 1024 * 1024

N_Q_HEADS = 32
N_KV_HEADS = 8
HEAD_DIM = 128
REP = N_Q_HEADS // N_KV_HEADS
WINDOW = 128
ROPE_DIM = HEAD_DIM // 4
ROPE_HALF = ROPE_DIM // 2
ROPE_THETA = 500000.0
SSM_GROUP = 16
SSM_STATE = 64
N_MOD = 6
EPS = 1e-6
MASK_VALUE = -1e30
CHUNK = MXU_DIM // SSM_GROUP


def _to_phase_major(v):
    return jnp.swapaxes(v.reshape(CHUNK, CHUNK, v.shape[-1]), 0, 1)


def _lagged_tiles(n_tiles, n_inner):
    def cur(step):
        t = jnp.minimum(step, n_tiles - 1)
        return t // n_inner, t % n_inner

    def prev(step):
        t = jnp.maximum(step - 1, 0)
        return t // n_inner, t % n_inner

    return cur, prev


def _overlapped_step(step, run_matmul, epilogues, accs):
    cases = [(step == 0, None)] + [((step > 0) & cond, fn) for cond, fn in epilogues]
    for parity in (0, 1):
        for cond, fn in cases:
            @pl.when(cond & (step % 2 == parity))
            def _(dst=accs[parity], src=accs[1 - parity], fn=fn):
                if fn is not None:
                    fn(src)
                run_matmul(dst)


def _cparams(*sem):
    return pltpu.CompilerParams(dimension_semantics=sem, vmem_limit_bytes=VMEM_LIMIT_BYTES)


def _mod_kernel(c_ref, w_ref, b_ref, o_ref, cb_ref, *, rk):
    d, tn = w_ref.shape

    def body(r, acc):
        rows = pl.ds(pl.multiple_of(r * rk, rk), rk)
        c = c_ref[rows, :]
        ca = c * jax.nn.sigmoid(c)
        cb_ref[rows, :] = jnp.broadcast_to(ca, (rk, LANES))
        prod = w_ref[rows, :] * ca
        return acc + jnp.sum(prod.reshape(rk // SUBLANES, SUBLANES, tn), axis=0)

    acc = lax.fori_loop(0, d // rk, body, jnp.zeros((SUBLANES, tn), F32))
    o_ref[...] = jnp.sum(acc, axis=0, keepdims=True) + b_ref[...]


def _modulation(c_col, w_ada, b_ada_row, n_cols, *, tn=512, rk=512):
    d = w_ada.shape[0]
    return pl.pallas_call(
        functools.partial(_mod_kernel, rk=rk),
        grid=(n_cols // tn,),
        in_specs=[
            pl.BlockSpec((d, 1), lambda j: (0, 0)),
            pl.BlockSpec((d, tn), lambda j: (0, j)),
            pl.BlockSpec((1, tn), lambda j: (0, j)),
        ],
        out_specs=[pl.BlockSpec((1, tn), lambda j: (0, j)), pl.BlockSpec((d, LANES), lambda j: (0, 0))],
        out_shape=[jax.ShapeDtypeStruct((1, n_cols), F32), jax.ShapeDtypeStruct((d, LANES), F32)],
        compiler_params=_cparams("arbitrary"),
        name="mod",
    )(c_col, w_ada, b_ada_row)


def _mod_side_task(cb_ref, wm_ref, bm_ref, om_ref):
    d, wc = wm_ref.shape
    cb = cb_ref[...]
    for q in range(wc // LANES):
        cols = slice(q * LANES, (q + 1) * LANES)
        part = jnp.sum((wm_ref[:, cols] * cb).reshape(d // SUBLANES, SUBLANES, LANES), axis=0)
        om_ref[:, cols] = jnp.sum(part, axis=0, keepdims=True) + bm_ref[:, cols]


def _norm_kernel(x_ref, g_ref, sh_ref, sc_ref, o_ref, *phase_ref):
    def normed(x):
        ms = jnp.mean(x * x, axis=-1, keepdims=True)
        y = x * lax.rsqrt(ms + EPS) * g_ref[...]
        return (y * (1.0 + sc_ref[...]) + sh_ref[...]).astype(o_ref.dtype)

    hb = normed(x_ref[...])
    o_ref[...] = hb
    for oph_ref in phase_ref:
        grp = CHUNK * CHUNK
        for q in range(hb.shape[0] // grp):
            oph_ref[:, q * CHUNK:(q + 1) * CHUNK, :] = _to_phase_major(hb[q * grp:(q + 1) * grp])


def _norm_mod(x, g_row, mod, shift_idx, *, with_phase_major=False, tm=512):
    s, d = x.shape
    out_specs = [pl.BlockSpec((tm, d), lambda i: (i, 0))]
    out_shape = [jax.ShapeDtypeStruct((s, d), BF16)]
    if with_phase_major:
        out_specs.append(pl.BlockSpec((CHUNK, tm // CHUNK, d), lambda i: (0, i, 0)))
        out_shape.append(jax.ShapeDtypeStruct((CHUNK, s // CHUNK, d), BF16))
    return pl.pallas_call(
        _norm_kernel,
        grid=(s // tm,),
        in_specs=[
            pl.BlockSpec((tm, d), lambda i: (i, 0)),
            pl.BlockSpec((1, d), lambda i: (0, 0)),
            pl.BlockSpec((1, d), lambda i: (0, shift_idx)),
            pl.BlockSpec((1, d), lambda i: (0, shift_idx + 1)),
        ],
        out_specs=out_specs,
        out_shape=out_shape,
        compiler_params=_cparams("arbitrary"),
        name="norm",
    )(x, g_row, mod, mod)


def _rope_kernel(pos_ref, c_ref, sa_ref, sb_ref):
    pos = pos_ref[...].astype(F32)
    lane = lax.broadcasted_iota(jnp.int32, (1, LANES), 1)
    idx = (lane & (ROPE_HALF - 1)).astype(F32)
    inv_freq = jnp.power(jnp.float32(ROPE_THETA), -2.0 * idx / ROPE_DIM)
    ang = pos * inv_freq
    cos = jnp.cos(ang)
    sin = jnp.sin(ang)
    c_ref[...] = jnp.where(lane < ROPE_DIM, cos, 1.0)
    sa_ref[...] = jnp.where(lane < ROPE_HALF, -sin, 0.0)
    sb_ref[...] = jnp.where((lane >= ROPE_HALF) & (lane < ROPE_DIM), sin, 0.0)


def _rope_tables(pos_col, *, tm=1024):
    s = pos_col.shape[0]
    tm = min(tm, s)
    spec = pl.BlockSpec((tm, LANES), lambda i: (i, 0))
    shp = jax.ShapeDtypeStruct((s, LANES), F32)
    return pl.pallas_call(
        _rope_kernel,
        grid=(s // tm,),
        in_specs=[pl.BlockSpec((tm, 1), lambda i: (i, 0))],
        out_specs=[spec, spec, spec],
        out_shape=[shp, shp, shp],
        compiler_params=_cparams("arbitrary"),
        name="rope",
    )(pos_col)


def _proj_kernel(h_ref, w_ref, qg_ref, kg_ref, rc_ref, rsa_ref, rsb_ref, cb_ref, wm_ref, bm_ref, o_ref, om_ref,
                 acc0_ref, acc1_ref, *, n_j, n_q, n_qk, n_qkv):
    step = pl.program_id(0)
    jp = (step - 1) % n_j
    tn = o_ref.shape[1]

    def qk_epilogue(src):
        g = jnp.where(jp < n_q, qg_ref[...], kg_ref[...])
        c, sa, sb = rc_ref[...], rsa_ref[...], rsb_ref[...]
        for hh in range(tn // HEAD_DIM):
            cols = slice(hh * HEAD_DIM, (hh + 1) * HEAD_DIM)
            xh = src[:, cols]
            ms = jnp.mean(xh * xh, axis=-1, keepdims=True)
            y = xh * lax.rsqrt(ms + EPS) * g
            y = y * c + pltpu.roll(y, HEAD_DIM - ROPE_HALF, 1) * sa + pltpu.roll(y, ROPE_HALF, 1) * sb
            o_ref[:, cols] = y.astype(o_ref.dtype)

    def v_epilogue(src):
        o_ref[...] = src[...].astype(o_ref.dtype)

    def gate_epilogue(src):
        o_ref[...] = jax.nn.sigmoid(src[...]).astype(o_ref.dtype)

    def run_matmul(dst):
        _mod_side_task(cb_ref, wm_ref, bm_ref, om_ref)
        dst[...] = jnp.dot(h_ref[...], w_ref[...].astype(BF16), preferred_element_type=F32)

    _overlapped_step(step, run_matmul,
                     [(jp < n_qk, qk_epilogue), ((jp >= n_qk) & (jp < n_qkv), v_epilogue), (jp >= n_qkv, gate_epilogue)],
                     (acc0_ref, acc1_ref))


def _proj(h, w_in, qg_row, kg_row, rope_c, rope_sa, rope_sb, c_bcast, w_ada, b_ada_row, mod_col0, *,
          d_model, kv_width, ssm_width, tm=1024, tn=512):
    s, d = h.shape
    tm = min(tm, s)
    n_q = d_model // tn
    n_qk = n_q + kv_width // tn
    n_qkv = n_qk + kv_width // tn
    n_u = ssm_width // tn
    n_j = n_qkv + d_model // tn
    n_tiles = (s // tm) * n_j
    cur, prev = _lagged_tiles(n_tiles, n_j)

    def w_map(t):
        j = cur(t)[1]
        return 0, jnp.where(j < n_qkv, j, j + n_u)

    n_mod = w_ada.shape[1] - mod_col0
    wc = LANES * pl.cdiv(n_mod // LANES, n_tiles)
    n_side = n_mod // wc
    assert n_mod % wc == 0 and mod_col0 % wc == 0

    def side(t):
        return jnp.minimum(t, n_side - 1)

    tab = pl.BlockSpec((tm, LANES), lambda t: (prev(t)[0], 0))
    row = pl.BlockSpec((1, HEAD_DIM), lambda t: (0, 0))
    return pl.pallas_call(
        functools.partial(_proj_kernel, n_j=n_j, n_q=n_q, n_qk=n_qk, n_qkv=n_qkv),
        grid=(n_tiles + 1,),
        in_specs=[
            pl.BlockSpec((tm, d), lambda t: (cur(t)[0], 0)),
            pl.BlockSpec((d, tn), w_map),
            row, row, tab, tab, tab,
            pl.BlockSpec((d, LANES), lambda t: (0, 0)),
            pl.BlockSpec((d, wc), lambda t: (0, mod_col0 // wc + side(t))),
            pl.BlockSpec((1, wc), lambda t: (0, mod_col0 // wc + side(t))),
        ],
        out_specs=[pl.BlockSpec((tm, tn), lambda t: prev(t)), pl.BlockSpec((1, wc), lambda t: (0, side(t)))],
        out_shape=[jax.ShapeDtypeStruct((s, n_j * tn), BF16), jax.ShapeDtypeStruct((1, n_mod), F32)],
        scratch_shapes=[pltpu.VMEM((tm, tn), F32), pltpu.VMEM((tm, tn), F32)],
        compiler_params=_cparams("arbitrary"),
        name="proj",
    )(h, w_in, qg_row, kg_row, rope_c, rope_sa, rope_sb, c_bcast, w_ada, b_ada_row)


def _proju_kernel(h_ref, w_ref, ut_ref, gs_ref, acc0_ref, acc1_ref, *, n_j, n_u):
    step = pl.program_id(0)
    jp = (step - 1) % n_j
    n_ph, nc, d = h_ref.shape

    def u_epilogue(src):
        for p in range(n_ph):
            ut_ref[p] = src[p * nc:(p + 1) * nc, :].T

    def gate_epilogue(src):
        for p in range(n_ph):
            gs_ref[p] = jax.nn.sigmoid(src[p * nc:(p + 1) * nc, :]).astype(gs_ref.dtype)

    def run_matmul(dst):
        dst[...] = jnp.dot(h_ref[...].reshape(n_ph * nc, d), w_ref[...].astype(BF16), preferred_element_type=F32)

    _overlapped_step(step, run_matmul, [(jp < n_u, u_epilogue), (jp >= n_u, gate_epilogue)], (acc0_ref, acc1_ref))


def _proj_u(h_phase, w_in, *, u_col0, gs_col0, ssm_width, d_model, phases=2, tn=512):
    _, nc, d = h_phase.shape
    n_u = ssm_width // tn
    n_j = n_u + d_model // tn
    n_tiles = (CHUNK // phases) * n_j
    cur, prev = _lagged_tiles(n_tiles, n_j)

    def w_map(s):
        j = cur(s)[1]
        return 0, jnp.where(j < n_u, u_col0 // tn + j, gs_col0 // tn + j - n_u)

    def ut_map(s):
        t, j = prev(s)
        return t, jnp.minimum(j, n_u - 1), 0

    def gs_map(s):
        t, j = prev(s)
        return t, 0, jnp.maximum(j - n_u, 0)

    return pl.pallas_call(
        functools.partial(_proju_kernel, n_j=n_j, n_u=n_u),
        grid=(n_tiles + 1,),
        in_specs=[
            pl.BlockSpec((phases, nc, d), lambda s: (cur(s)[0], 0, 0)),
            pl.BlockSpec((d, tn), w_map),
        ],
        out_specs=[
            pl.BlockSpec((phases, tn, nc), ut_map),
            pl.BlockSpec((phases, nc, tn), gs_map),
        ],
        out_shape=[jax.ShapeDtypeStruct((CHUNK, ssm_width, nc), F32),
                   jax.ShapeDtypeStruct((CHUNK, nc, d_model), BF16)],
        scratch_shapes=[pltpu.VMEM((phases * nc, tn), F32), pltpu.VMEM((phases * nc, tn), F32)],
        compiler_params=_cparams("arbitrary"),
        name="proj_u",
    )(h_phase, w_in)


def _cmul(ar, ai, br, bi):
    return ar * br - ai * bi, ar * bi + ai * br


def _s5prep_kernel(ldt_ref, lr_ref, li_ref, bre_ref, bim_ref, cre_ref, cim_ref, m_ref, w_ref, v_ref, ap_ref, *,
                   n_levels):
    for gi in range(m_ref.shape[0]):
        _s5prep_group(gi, ldt_ref, lr_ref, li_ref, bre_ref, bim_ref, cre_ref, cim_ref, m_ref, w_ref, v_ref, ap_ref,
                      n_levels)


def _s5prep_group(gi, ldt_ref, lr_ref, li_ref, bre_ref, bim_ref, cre_ref, cim_ref, m_ref, w_ref, v_ref, ap_ref,
                  n_levels):
    p_ = SSM_STATE
    kdim = CHUNK * SSM_GROUP
    first = lax.broadcasted_iota(jnp.int32, (1, 2 * p_), 1) < p_
    dt = jnp.exp(ldt_ref[gi])
    lr, li = lr_ref[gi], li_ref[gi]
    mag = jnp.exp(lr * dt)
    are, aim = mag * jnp.cos(li * dt), mag * jnp.sin(li * dt)
    den = lr * lr + li * li
    nr = are - 1.0
    coef_re = (nr * lr + aim * li) / den
    coef_im = (aim * lr - nr * li) / den
    bt_re, bt_im = bre_ref[gi], bim_ref[gi]
    bbar_re = coef_re * bt_re - coef_im * bt_im
    bbar_im = coef_re * bt_im + coef_im * bt_re

    pow_re = [jnp.ones_like(are)]
    pow_im = [jnp.zeros_like(are)]
    for _ in range(CHUNK):
        nre, nim = _cmul(pow_re[-1], pow_im[-1], are, aim)
        pow_re.append(nre)
        pow_im.append(nim)

    blocks = []
    for m in range(CHUNK):
        g_re, g_im = _cmul(pow_re[CHUNK - 1 - m], pow_im[CHUNK - 1 - m], bbar_re, bbar_im)
        blocks.append(jnp.where(first, g_re, g_im))
    g_t = jnp.concatenate(blocks, axis=0)
    w_ref[gi] = g_t.T.astype(w_ref.dtype)

    cre2, cim2 = cre_ref[gi], cim_ref[gi]
    c_mix = jnp.where(first, cre2, -cim2)
    k_rev = lax.dot_general(c_mix, g_t, (((1,), (1,)), ((), ())), precision=lax.Precision.HIGHEST,
                            preferred_element_type=F32)
    k_ext = jnp.concatenate([k_rev, jnp.zeros_like(k_rev)], axis=1)
    for t in range(CHUNK):
        off = (CHUNK - 1 - t) * SSM_GROUP
        m_ref[gi, t * SSM_GROUP:(t + 1) * SSM_GROUP, :] = k_ext[:, off:off + kdim].astype(m_ref.dtype)

    for t in range(CHUNK):
        pr, pi = pow_re[t + 1], pow_im[t + 1]
        blk_v = jnp.where(first, cre2 * pr - cim2 * pi, -(cre2 * pi + cim2 * pr))
        v_ref[gi, t * SSM_GROUP:(t + 1) * SSM_GROUP, :] = blk_v.astype(v_ref.dtype)

    sub = lax.broadcasted_iota(jnp.int32, (2 * p_, 1), 0)
    qr, qi = pow_re[CHUNK], pow_im[CHUNK]
    levels = jnp.zeros((2 * p_, 2 * p_), F32)
    for k in range(n_levels):
        levels = jnp.where(sub == k, qr, levels)
        levels = jnp.where(sub == p_ + k, qi, levels)
        qr, qi = _cmul(qr, qi, qr, qi)
    ap_ref[gi] = levels.T[:p_]


def _s5_prep(log_dt, lam_re, lam_im, b_re, b_im, c_re, c_im, *, n_levels, gb=4):
    g, p_ = lam_re.shape

    def dup(a):
        return jnp.concatenate([a, a], axis=-1)

    ldt = log_dt.reshape(g, 1, 1)
    lr, li = dup(lam_re).reshape(g, 1, 2 * p_), dup(lam_im).reshape(g, 1, 2 * p_)
    bt_re, bt_im = dup(jnp.swapaxes(b_re, 1, 2)), dup(jnp.swapaxes(b_im, 1, 2))
    kdim = CHUNK * SSM_GROUP

    def spec(*shape):
        return pl.BlockSpec((gb,) + shape, lambda i: (i, 0, 0))

    return pl.pallas_call(
        functools.partial(_s5prep_kernel, n_levels=n_levels),
        grid=(g // gb,),
        in_specs=[spec(1, 1), spec(1, 2 * p_), spec(1, 2 * p_), spec(SSM_GROUP, 2 * p_), spec(SSM_GROUP, 2 * p_),
                  spec(SSM_GROUP, 2 * p_), spec(SSM_GROUP, 2 * p_)],
        out_specs=[spec(kdim, kdim), spec(2 * p_, kdim), spec(kdim, 2 * p_), spec(p_, 2 * p_)],
        out_shape=[jax.ShapeDtypeStruct((g, kdim, kdim), BF16),
                   jax.ShapeDtypeStruct((g, 2 * p_, kdim), BF16),
                   jax.ShapeDtypeStruct((g, kdim, 2 * p_), BF16),
                   jax.ShapeDtypeStruct((g, p_, 2 * p_), F32)],
        compiler_params=_cparams("arbitrary"),
        name="s5prep",
    )(ldt, lr, li, bt_re, bt_im, dup(c_re), dup(c_im))


def _s5_kernel(x_ref, m_ref, w_ref, v_ref, ap_ref, d_ref, o_ref, *, n_levels):
    for gi in range(m_ref.shape[0]):
        _s5_group(gi, x_ref, m_ref, w_ref, v_ref, ap_ref, d_ref, o_ref, n_levels)


def _s5_group(gi, x_ref, m_ref, w_ref, v_ref, ap_ref, d_ref, o_ref, n_levels):
    p_ = SSM_STATE
    kdim = CHUNK * SSM_GROUP
    nc = x_ref.shape[-1]
    chans = slice(gi * SSM_GROUP, (gi + 1) * SSM_GROUP)
    x = x_ref[:, chans, :].reshape(kdim, nc)
    xb = x.astype(BF16)
    y = jnp.dot(m_ref[gi], xb, preferred_element_type=F32)
    loc = jnp.dot(w_ref[gi], xb, preferred_element_type=F32)
    s_re, s_im = loc[:p_], loc[p_:]
    lane = lax.broadcasted_iota(jnp.int32, (1, nc), 1)
    ap = ap_ref[gi]
    for k in range(n_levels):
        sh = 1 << k
        pr, pi = ap[:, k:k + 1], ap[:, p_ + k:p_ + k + 1]
        keep = lane >= sh
        t_re = jnp.where(keep, pltpu.roll(s_re, sh, 1), 0.0)
        t_im = jnp.where(keep, pltpu.roll(s_im, sh, 1), 0.0)
        s_re, s_im = s_re + (pr * t_re - pi * t_im), s_im + (pr * t_im + pi * t_re)
    keep = lane >= 1
    start = jnp.concatenate([jnp.where(keep, pltpu.roll(s_re, 1, 1), 0.0),
                             jnp.where(keep, pltpu.roll(s_im, 1, 1), 0.0)], axis=0)
    y = y + jnp.dot(v_ref[gi], start.astype(BF16), preferred_element_type=F32)
    y = y + d_ref[gi] * x
    o_ref[:, chans, :] = jax.nn.gelu(y).astype(o_ref.dtype).reshape(CHUNK, SSM_GROUP, nc)


def _s5_scan(u_t, m_op, w_op, v_op, a_pow, d_col, *, n_levels, gb=2):
    _, width, nc = u_t.shape
    g = width // SSM_GROUP
    kdim = CHUNK * SSM_GROUP
    p2 = 2 * SSM_STATE

    def spec(*shape):
        return pl.BlockSpec((gb,) + shape, lambda i: (i, 0, 0))

    io = pl.BlockSpec((CHUNK, gb * SSM_GROUP, nc), lambda i: (0, i, 0))
    return pl.pallas_call(
        functools.partial(_s5_kernel, n_levels=n_levels),
        grid=(g // gb,),
        in_specs=[io, spec(kdim, kdim), spec(p2, kdim), spec(kdim, p2), spec(SSM_STATE, p2), spec(kdim, 1)],
        out_specs=io,
        out_shape=jax.ShapeDtypeStruct((CHUNK, width, nc), BF16),
        compiler_params=_cparams("arbitrary"),
        name="s5",
    )(u_t, m_op, w_op, v_op, a_pow, d_col)


def _attn_kernel(sink_ref, q_ref, kc_ref, kp_ref, vc_ref, vp_ref, ga_ref, o_ref, nat_ref, *, n_groups):
    i = pl.program_id(0)
    gh = pl.program_id(1)
    tq = q_ref.shape[0]
    nb = tq // WINDOW
    scale = HEAD_DIM ** -0.5
    row = lax.broadcasted_iota(jnp.int32, (WINDOW, WINDOW), 0)
    col = lax.broadcasted_iota(jnp.int32, (WINDOW, WINDOW), 1)
    tri1 = col <= row
    tri = jnp.concatenate([tri1] * REP, axis=0)
    nt = (((1,), (1,)), ((), ()))
    for b in range(nb):
        rows = slice(b * WINDOW, (b + 1) * WINDOW)
        has_prev = (i * nb + b) > 0
        valid = tri | has_prev
        for g in range(n_groups):
            kcols = slice(g * HEAD_DIM, (g + 1) * HEAD_DIM)
            if b == 0:
                k_prev, v_prev = kp_ref[:, kcols], vp_ref[:, kcols]
            else:
                prows = slice((b - 1) * WINDOW, b * WINDOW)
                k_prev, v_prev = kc_ref[prows, kcols], vc_ref[prows, kcols]
            k_band = jnp.concatenate([k_prev, kc_ref[rows, kcols]], axis=0)
            v_band = jnp.concatenate([v_prev, vc_ref[rows, kcols]], axis=0)
            qcols = [slice((g * REP + r) * HEAD_DIM, (g * REP + r + 1) * HEAD_DIM) for r in range(REP)]
            qg = jnp.concatenate([q_ref[rows, c] for c in qcols], axis=0)
            s2 = lax.dot_general(qg, k_band, nt, preferred_element_type=F32)
            s = jnp.where(tri, s2[:, WINDOW:], s2[:, :WINDOW]) * scale
            s = jnp.where(valid, s, MASK_VALUE)
            probs = []
            for r in range(REP):
                sink = sink_ref[(gh * n_groups + g) * REP + r]
                sr = s[r * WINDOW:(r + 1) * WINDOW]
                m = jnp.maximum(jnp.max(sr, axis=-1, keepdims=True), sink)
                p = jnp.exp(sr - m)
                denom = jnp.sum(p, axis=-1, keepdims=True) + jnp.exp(sink - m)
                probs.append(p * (1.0 / denom))
            pr = jnp.concatenate(probs, axis=0)
            p2 = jnp.concatenate([jnp.where(tri, 0.0, pr), jnp.where(tri, pr, 0.0)], axis=1).astype(BF16)
            o = jnp.dot(p2, v_band, preferred_element_type=F32)
            for r in range(REP):
                gate = ga_ref[rows, qcols[r]].astype(F32)
                nat_ref[rows, qcols[r]] = gate * o[r * WINDOW:(r + 1) * WINDOW]
    grp = CHUNK * CHUNK
    for q in range(tq // grp):
        nat = nat_ref[q * grp:(q + 1) * grp, :].astype(o_ref.dtype)
        o_ref[:, q * CHUNK:(q + 1) * CHUNK, :] = _to_phase_major(nat)


def _attention(sinks, pz, *, d_model, kv_width, tq=256, n_groups=4):
    s = pz.shape[0]
    tq = min(tq, s)
    nb = tq // WINDOW
    wq = n_groups * REP * HEAD_DIM
    wk = n_groups * HEAD_DIM
    k0 = d_model // wk
    v0 = (d_model + kv_width) // wk
    a0 = (d_model + 2 * kv_width) // wq

    def prev_map(col0):
        return lambda i, gh: (jnp.maximum(i * nb - 1, 0), col0 + gh)

    return pl.pallas_call(
        functools.partial(_attn_kernel, n_groups=n_groups),
        grid=(s // tq, d_model // wq),
        in_specs=[
            pl.BlockSpec(memory_space=pltpu.SMEM),
            pl.BlockSpec((tq, wq), lambda i, gh: (i, gh)),
            pl.BlockSpec((tq, wk), lambda i, gh: (i, k0 + gh)),
            pl.BlockSpec((WINDOW, wk), prev_map(k0)),
            pl.BlockSpec((tq, wk), lambda i, gh: (i, v0 + gh)),
            pl.BlockSpec((WINDOW, wk), prev_map(v0)),
            pl.BlockSpec((tq, wq), lambda i, gh: (i, a0 + gh)),
        ],
        out_specs=pl.BlockSpec((CHUNK, tq // CHUNK, wq), lambda i, gh: (0, i, gh)),
        out_shape=jax.ShapeDtypeStruct((CHUNK, s // CHUNK, d_model), BF16),
        scratch_shapes=[pltpu.VMEM((tq, wq), F32)],
        compiler_params=_cparams("arbitrary", "arbitrary"),
        name="attn",
    )(sinks, pz, pz, pz, pz, pz, pz)


def _glu_kernel(y_ref, wa_ref, wb_ref, at_ref, gs_ref, o_ref, yt_ref):
    n_ph, _, nc = y_ref.shape

    @pl.when(pl.program_id(1) == 0)
    def _():
        for p in range(n_ph):
            yt_ref[p * nc:(p + 1) * nc, :] = y_ref[p].T

    y = yt_ref[...]
    a = jnp.dot(y, wa_ref[...].astype(BF16), preferred_element_type=F32)
    b = jnp.dot(y, wb_ref[...].astype(BF16), preferred_element_type=F32)
    branch = a * jax.nn.sigmoid(b)
    for p in range(n_ph):
        rows = slice(p * nc, (p + 1) * nc)
        o_ref[p] = (at_ref[p].astype(F32) + gs_ref[p].astype(F32) * branch[rows]).astype(o_ref.dtype)


def _glu_merge(y_t, w_glu, attn_phase, gs_phase, *, phases=2, tn=512):
    _, width, nc = y_t.shape
    d_model = attn_phase.shape[-1]
    nj = d_model // tn
    blk = pl.BlockSpec((phases, nc, tn), lambda t, j: (t, 0, j))
    return pl.pallas_call(
        _glu_kernel,
        grid=(CHUNK // phases, nj),
        in_specs=[
            pl.BlockSpec((phases, width, nc), lambda t, j: (t, 0, 0)),
            pl.BlockSpec((width, tn), lambda t, j: (0, j)),
            pl.BlockSpec((width, tn), lambda t, j: (0, nj + j)),
            blk, blk,
        ],
        out_specs=blk,
        out_shape=jax.ShapeDtypeStruct(attn_phase.shape, BF16),
        scratch_shapes=[pltpu.VMEM((phases * nc, width), BF16)],
        compiler_params=_cparams("arbitrary", "arbitrary"),
        name="glu",
    )(y_t, w_glu, w_glu, attn_phase, gs_phase)


def _outp_kernel(a_ref, w_ref, x_ref, g_ref, o_ref, nat_ref):
    grp = CHUNK * CHUNK

    @pl.when(pl.program_id(1) == 0)
    def _():
        for q in range(nat_ref.shape[0] // grp):
            a = jnp.swapaxes(a_ref[:, q * CHUNK:(q + 1) * CHUNK, :], 0, 1)
            nat_ref[q * grp:(q + 1) * grp, :] = a.reshape(grp, a_ref.shape[-1])

    acc = jnp.dot(nat_ref[...], w_ref[...].astype(BF16), preferred_element_type=F32)
    o_ref[...] = x_ref[...] + g_ref[...] * acc


def _out_proj(a_phase, w, x, mod, gate_idx, *, tm=1024, tn=512):
    n_ph, nc, k = a_phase.shape
    s, n = x.shape
    tm = min(tm, s)
    return pl.pallas_call(
        _outp_kernel,
        grid=(s // tm, n // tn),
        in_specs=[
            pl.BlockSpec((n_ph, tm // n_ph, k), lambda i, j: (0, i, 0)),
            pl.BlockSpec((k, tn), lambda i, j: (0, j)),
            pl.BlockSpec((tm, tn), lambda i, j: (i, j)),
            pl.BlockSpec((1, tn), lambda i, j: (0, gate_idx * (n // tn) + j)),
        ],
        out_specs=pl.BlockSpec((tm, tn), lambda i, j: (i, j)),
        out_shape=jax.ShapeDtypeStruct((s, n), F32),
        scratch_shapes=[pltpu.VMEM((tm, k), BF16)],
        compiler_params=_cparams("arbitrary", "arbitrary"),
        name="outp",
    )(a_phase, w, x, mod)


def _ff1_kernel(h_ref, w_ref, o_ref):
    acc = jnp.dot(h_ref[...], w_ref[...].astype(BF16), preferred_element_type=F32)
    r = jnp.maximum(acc, 0.0)
    o_ref[...] = (r * r).astype(o_ref.dtype)


def _ff1(h, w, *, tm=1024, tn=512):
    s, k = h.shape
    n = w.shape[1]
    tm = min(tm, s)
    return pl.pallas_call(
        _ff1_kernel,
        grid=(s // tm, n // tn),
        in_specs=[pl.BlockSpec((tm, k), lambda i, j: (i, 0)), pl.BlockSpec((k, tn), lambda i, j: (0, j))],
        out_specs=pl.BlockSpec((tm, tn), lambda i, j: (i, j)),
        out_shape=jax.ShapeDtypeStruct((s, n), BF16),
        compiler_params=_cparams("arbitrary", "arbitrary"),
        name="ff1",
    )(h, w)


def _ff2_kernel(a_ref, w_ref, x_ref, g_ref, o_ref):
    kk = pl.program_id(2)
    last = pl.num_programs(2) - 1

    def product():
        return jnp.dot(a_ref[...], w_ref[...].astype(BF16), preferred_element_type=F32)

    @pl.when(kk == 0)
    def _():
        o_ref[...] = product()

    @pl.when((kk > 0) & (kk < last))
    def _():
        o_ref[...] += product()

    @pl.when(kk == last)
    def _():
        o_ref[...] = x_ref[...] + g_ref[...] * (o_ref[...] + product())


def _ff2(a, w, x, mod, gate_idx, *, tm=2048, tn=1024, tk=1024):
    s, k = a.shape
    n = w.shape[1]
    tm = min(tm, s)
    assert k // tk >= 2
    return pl.pallas_call(
        _ff2_kernel,
        grid=(s // tm, n // tn, k // tk),
        in_specs=[
            pl.BlockSpec((tm, tk), lambda i, j, kk: (i, kk)),
            pl.BlockSpec((tk, tn), lambda i, j, kk: (kk, j)),
            pl.BlockSpec((tm, tn), lambda i, j, kk: (i, j)),
            pl.BlockSpec((1, tn), lambda i, j, kk: (0, gate_idx * (n // tn) + j)),
        ],
        out_specs=pl.BlockSpec((tm, tn), lambda i, j, kk: (i, j)),
        out_shape=jax.ShapeDtypeStruct((s, n), F32),
        compiler_params=_cparams("arbitrary", "arbitrary", "arbitrary"),
        name="ff2",
    )(a, w, x, mod)


def _layer(x, c_col, pos_col, w_ada, b_ada, norm1_g, norm2_g, w_in, q_norm_g, k_norm_g, attn_sinks,
           lam_re, lam_im, log_dt, b_re, b_im, c_re, c_im, ssm_d, w_glu, w_out, w_ff1, w_ff2):
    s, d = x.shape
    kv_width = N_KV_HEADS * HEAD_DIM
    ssm_width = ssm_d.shape[0]
    n_groups = ssm_width // SSM_GROUP
    nc = s // CHUNK
    n_levels = max(1, (nc - 1).bit_length())
    u_col0 = d + 2 * kv_width

    b_row = b_ada.reshape(1, -1)
    mod_a, c_bcast = _modulation(c_col, w_ada, b_row, 2 * d)
    h, h_phase = _norm_mod(x, norm1_g.reshape(1, d), mod_a, 0, with_phase_major=True)
    rope_c, rope_sa, rope_sb = _rope_tables(pos_col)
    pz, mod_b = _proj(h, w_in, q_norm_g.reshape(1, -1), k_norm_g.reshape(1, -1), rope_c, rope_sa, rope_sb,
                      c_bcast, w_ada, b_row, 2 * d, d_model=d, kv_width=kv_width, ssm_width=ssm_width)
    u_t, gs_phase = _proj_u(h_phase, w_in, u_col0=u_col0, gs_col0=u_col0 + ssm_width + d, ssm_width=ssm_width,
                            d_model=d)

    m_op, w_op, v_op, a_pow = _s5_prep(log_dt, lam_re, lam_im, b_re, b_im, c_re, c_im, n_levels=n_levels)
    d_col = jnp.tile(ssm_d.reshape(n_groups, 1, SSM_GROUP), (1, CHUNK, 1)).reshape(n_groups, CHUNK * SSM_GROUP, 1)
    y_t = _s5_scan(u_t, m_op, w_op, v_op, a_pow, d_col, n_levels=n_levels)

    attn_phase = _attention(attn_sinks, pz, d_model=d, kv_width=kv_width)
    merged_phase = _glu_merge(y_t, w_glu, attn_phase, gs_phase)
    x1 = _out_proj(merged_phase, w_out, x, mod_b, 0)

    (h2,) = _norm_mod(x1, norm2_g.reshape(1, d), mod_b, 1)
    act = _ff1(h2, w_ff1)
    return _ff2(act, w_ff2, x1, mod_b, 3)


def kernel(x, c, positions, w_ada, b_ada, norm1_g, norm2_g, w_in, q_norm_g, k_norm_g, attn_sinks, ssm_lam_re,
           ssm_lam_im, ssm_log_dt, ssm_b_re, ssm_b_im, ssm_c_re, ssm_c_im, ssm_d, w_glu, w_out, w_ff1, w_ff2):
    bsz, s, d = x.shape
    assert bsz == 1 and d == N_Q_HEADS * HEAD_DIM and s % (CHUNK * LANES) == 0
    xs = x[0]
    c_col = c.reshape(d, 1)
    pos_col = positions.reshape(s, 1)
    for l in range(w_ada.shape[0]):
        xs = _layer(xs, c_col, pos_col, w_ada[l], b_ada[l], norm1_g[l], norm2_g[l], w_in[l], q_norm_g[l],
                    k_norm_g[l], attn_sinks[l], ssm_lam_re[l], ssm_lam_im[l], ssm_log_dt[l], ssm_b_re[l],
                    ssm_b_im[l], ssm_c_re[l], ssm_c_im[l], ssm_d[l], w_glu[l], w_out[l], w_ff1[l], w_ff2[l])
    return xs[None]
```

```python
import functools
import math

import jax
import jax.numpy as jnp
from jax import lax
from jax.experimental import pallas as pl
from jax.experimental.pallas import tpu as pltpu

F32 = jnp.float32
BF16 = jnp.bfloat16

LANES = 128
SUBLANES = 8
MXU_DIM = 256
VMEM_LIMIT_BYTES = 56 * 1024 * 1024

N_Q_HEADS = 32
N_KV_HEADS = 8
HEAD_DIM = 128
REP = N_Q_HEADS // N_KV_HEADS
WINDOW = 128
ROPE_DIM = HEAD_DIM // 4
ROPE_HALF = ROPE_DIM // 2
ROPE_THETA = 500000.0
SSM_GROUP = 16
SSM_STATE = 64
N_MOD = 6
EPS = 1e-6
MASK_VALUE = -1e30
CHUNK = MXU_DIM // SSM_GROUP


def _to_phase_major(v):
    return jnp.swapaxes(v.reshape(CHUNK, CHUNK, v.shape[-1]), 0, 1)


def _lagged_tiles(n_tiles, n_inner):
    def cur(step):
        t = jnp.minimum(step, n_tiles - 1)
        return t // n_inner, t % n_inner

    def prev(step):
        t = jnp.maximum(step - 1, 0)
        return t // n_inner, t % n_inner

    return cur, prev


def _overlapped_step(step, run_matmul, epilogues, accs):
    cases = [(step == 0, None)] + [((step > 0) & cond, fn) for cond, fn in epilogues]
    for parity in (0, 1):
        for cond, fn in cases:
            @pl.when(cond & (step % 2 == parity))
            def _(dst=accs[parity], src=accs[1 - parity], fn=fn):
                if fn is not None:
                    fn(src)
                run_matmul(dst)


def _side_slabs(n_tiles, rows):
    n_side = 1 << (n_tiles.bit_length() - 1)
    assert rows % (n_side * 2 * SUBLANES) == 0
    return n_side, rows // n_side


def _cparams(*sem):
    return pltpu.CompilerParams(dimension_semantics=sem, vmem_limit_bytes=VMEM_LIMIT_BYTES)


def _mod_kernel(c_ref, w_ref, b_ref, o_ref, cb_ref, *, rk):
    d, tn = w_ref.shape

    def body(r, acc):
        rows = pl.ds(pl.multiple_of(r * rk, rk), rk)
        c = c_ref[rows, :]
        ca = c * jax.nn.sigmoid(c)
        cb_ref[rows, :] = jnp.broadcast_to(ca, (rk, LANES))
        prod = w_ref[rows, :] * ca
        return acc + jnp.sum(prod.reshape(rk // SUBLANES, SUBLANES, tn), axis=0)

    acc = lax.fori_loop(0, d // rk, body, jnp.zeros((SUBLANES, tn), F32))
    o_ref[...] = jnp.sum(acc, axis=0, keepdims=True) + b_ref[...]


def _modulation(c_col, w_ada, b_ada_row, n_cols, *, tn=512, rk=512):
    d = w_ada.shape[0]
    return pl.pallas_call(
        functools.partial(_mod_kernel, rk=rk),
        grid=(n_cols // tn,),
        in_specs=[
            pl.BlockSpec((d, 1), lambda j: (0, 0)),
            pl.BlockSpec((d, tn), lambda j: (0, j)),
            pl.BlockSpec((1, tn), lambda j: (0, j)),
        ],
        out_specs=[pl.BlockSpec((1, tn), lambda j: (0, j)), pl.BlockSpec((d, LANES), lambda j: (0, 0))],
        out_shape=[jax.ShapeDtypeStruct((1, n_cols), F32), jax.ShapeDtypeStruct((d, LANES), F32)],
        compiler_params=_cparams("arbitrary"),
        name="mod",
    )(c_col, w_ada, b_ada_row)


def _mod_side_task(cb_ref, wm_ref, bm_ref, om_ref):
    d, wc = wm_ref.shape
    cb = cb_ref[...]
    for q in range(wc // LANES):
        cols = slice(q * LANES, (q + 1) * LANES)
        part = jnp.sum((wm_ref[:, cols] * cb).reshape(d // SUBLANES, SUBLANES, LANES), axis=0)
        om_ref[:, cols] = jnp.sum(part, axis=0, keepdims=True) + bm_ref[:, cols]


def _norm_kernel(x_ref, g_ref, sh_ref, sc_ref, o_ref, *phase_ref):
    def normed(x):
        ms = jnp.mean(x * x, axis=-1, keepdims=True)
        y = x * lax.rsqrt(ms + EPS) * g_ref[...]
        return (y * (1.0 + sc_ref[...]) + sh_ref[...]).astype(o_ref.dtype)

    hb = normed(x_ref[...])
    o_ref[...] = hb
    for oph_ref in phase_ref:
        grp = CHUNK * CHUNK
        for q in range(hb.shape[0] // grp):
            oph_ref[:, q * CHUNK:(q + 1) * CHUNK, :] = _to_phase_major(hb[q * grp:(q + 1) * grp])


def _norm_mod(x, g_row, mod, shift_idx, *, with_phase_major=False, tm=512):
    s, d = x.shape
    out_specs = [pl.BlockSpec((tm, d), lambda i: (i, 0))]
    out_shape = [jax.ShapeDtypeStruct((s, d), BF16)]
    if with_phase_major:
        out_specs.append(pl.BlockSpec((CHUNK, tm // CHUNK, d), lambda i: (0, i, 0)))
        out_shape.append(jax.ShapeDtypeStruct((CHUNK, s // CHUNK, d), BF16))
    return pl.pallas_call(
        _norm_kernel,
        grid=(s // tm,),
        in_specs=[
            pl.BlockSpec((tm, d), lambda i: (i, 0)),
            pl.BlockSpec((1, d), lambda i: (0, 0)),
            pl.BlockSpec((1, d), lambda i: (0, shift_idx)),
            pl.BlockSpec((1, d), lambda i: (0, shift_idx + 1)),
        ],
        out_specs=out_specs,
        out_shape=out_shape,
        compiler_params=_cparams("arbitrary"),
        name="norm",
    )(x, g_row, mod, mod)


def _rope_kernel(pos_ref, c_ref, sa_ref, sb_ref):
    pos = pos_ref[...].astype(F32)
    lane = lax.broadcasted_iota(jnp.int32, (1, LANES), 1)
    idx = (lane & (ROPE_HALF - 1)).astype(F32)
    inv_freq = jnp.power(jnp.float32(ROPE_THETA), -2.0 * idx / ROPE_DIM)
    ang = pos * inv_freq
    cos = jnp.cos(ang)
    sin = jnp.sin(ang)
    c_ref[...] = jnp.where(lane < ROPE_DIM, cos, 1.0)
    sa_ref[...] = jnp.where(lane < ROPE_HALF, -sin, 0.0)
    sb_ref[...] = jnp.where((lane >= ROPE_HALF) & (lane < ROPE_DIM), sin, 0.0)


def _rope_tables(pos_col, *, tm=1024):
    s = pos_col.shape[0]
    tm = min(tm, s)
    spec = pl.BlockSpec((tm, LANES), lambda i: (i, 0))
    shp = jax.ShapeDtypeStruct((s, LANES), F32)
    return pl.pallas_call(
        _rope_kernel,
        grid=(s // tm,),
        in_specs=[pl.BlockSpec((tm, 1), lambda i: (i, 0))],
        out_specs=[spec, spec, spec],
        out_shape=[shp, shp, shp],
        compiler_params=_cparams("arbitrary"),
        name="rope",
    )(pos_col)


def _proj_kernel(h_ref, w_ref, qg_ref, kg_ref, rc_ref, rsa_ref, rsb_ref, cb_ref, wm_ref, bm_ref, wu_ref, wgs_ref,
                 o_ref, om_ref, wug_ref, acc0_ref, acc1_ref, *, n_j, n_q, n_qk, n_qkv):
    step = pl.program_id(0)
    jp = (step - 1) % n_j
    tn = o_ref.shape[1]

    def qk_epilogue(src):
        g = jnp.where(jp < n_q, qg_ref[...], kg_ref[...])
        c, sa, sb = rc_ref[...], rsa_ref[...], rsb_ref[...]
        for hh in range(tn // HEAD_DIM):
            cols = slice(hh * HEAD_DIM, (hh + 1) * HEAD_DIM)
            xh = src[:, cols]
            ms = jnp.mean(xh * xh, axis=-1, keepdims=True)
            y = xh * lax.rsqrt(ms + EPS) * g
            y = y * c + pltpu.roll(y, HEAD_DIM - ROPE_HALF, 1) * sa + pltpu.roll(y, ROPE_HALF, 1) * sb
            o_ref[:, cols] = y.astype(o_ref.dtype)

    def v_epilogue(src):
        o_ref[...] = src[...].astype(o_ref.dtype)

    def gate_epilogue(src):
        o_ref[...] = jax.nn.sigmoid(src[...]).astype(o_ref.dtype)

    def run_matmul(dst):
        _mod_side_task(cb_ref, wm_ref, bm_ref, om_ref)
        n_u_cols = wu_ref.shape[1]
        wug_ref[:, :n_u_cols] = wu_ref[...].astype(wug_ref.dtype)
        wug_ref[:, n_u_cols:] = wgs_ref[...].astype(wug_ref.dtype)
        dst[...] = jnp.dot(h_ref[...], w_ref[...].astype(BF16), preferred_element_type=F32)

    _overlapped_step(step, run_matmul,
                     [(jp < n_qk, qk_epilogue), ((jp >= n_qk) & (jp < n_qkv), v_epilogue), (jp >= n_qkv, gate_epilogue)],
                     (acc0_ref, acc1_ref))


def _proj(h, w_in, qg_row, kg_row, rope_c, rope_sa, rope_sb, c_bcast, w_ada, b_ada_row, mod_col0, *,
          d_model, kv_width, ssm_width, tm=1024, tn=512):
    s, d = h.shape
    tm = min(tm, s)
    n_q = d_model // tn
    n_qk = n_q + kv_width // tn
    n_qkv = n_qk + kv_width // tn
    n_u = ssm_width // tn
    n_j = n_qkv + d_model // tn
    n_tiles = (s // tm) * n_j
    cur, prev = _lagged_tiles(n_tiles, n_j)

    def w_map(t):
        j = cur(t)[1]
        return 0, jnp.where(j < n_qkv, j, j + n_u)

    n_mod = w_ada.shape[1] - mod_col0
    wc = LANES * pl.cdiv(n_mod // LANES, n_tiles)
    n_side = n_mod // wc
    assert n_mod % wc == 0 and mod_col0 % wc == 0

    def side(t):
        return jnp.minimum(t, n_side - 1)

    u_col0 = d_model + 2 * kv_width
    gs_col0 = u_col0 + ssm_width + d_model
    assert u_col0 % ssm_width == 0 and gs_col0 % d_model == 0
    n_cast, cast_rows = _side_slabs(n_tiles, d)

    def cast_map(col_block):
        return lambda t: (jnp.minimum(t, n_cast - 1), col_block)

    tab = pl.BlockSpec((tm, LANES), lambda t: (prev(t)[0], 0))
    row = pl.BlockSpec((1, HEAD_DIM), lambda t: (0, 0))
    return pl.pallas_call(
        functools.partial(_proj_kernel, n_j=n_j, n_q=n_q, n_qk=n_qk, n_qkv=n_qkv),
        grid=(n_tiles + 1,),
        in_specs=[
            pl.BlockSpec((tm, d), lambda t: (cur(t)[0], 0)),
            pl.BlockSpec((d, tn), w_map),
            row, row, tab, tab, tab,
            pl.BlockSpec((d, LANES), lambda t: (0, 0)),
            pl.BlockSpec((d, wc), lambda t: (0, mod_col0 // wc + side(t))),
            pl.BlockSpec((1, wc), lambda t: (0, mod_col0 // wc + side(t))),
            pl.BlockSpec((cast_rows, ssm_width), cast_map(u_col0 // ssm_width)),
            pl.BlockSpec((cast_rows, d_model), cast_map(gs_col0 // d_model)),
        ],
        out_specs=[pl.BlockSpec((tm, tn), lambda t: prev(t)), pl.BlockSpec((1, wc), lambda t: (0, side(t))),
                   pl.BlockSpec((cast_rows, ssm_width + d_model), cast_map(0))],
        out_shape=[jax.ShapeDtypeStruct((s, n_j * tn), BF16), jax.ShapeDtypeStruct((1, n_mod), F32),
                   jax.ShapeDtypeStruct((d, ssm_width + d_model), BF16)],
        scratch_shapes=[pltpu.VMEM((tm, tn), F32), pltpu.VMEM((tm, tn), F32)],
        compiler_params=_cparams("arbitrary"),
        name="proj",
    )(h, w_in, qg_row, kg_row, rope_c, rope_sa, rope_sb, c_bcast, w_ada, b_ada_row, w_in, w_in)


def _proju_kernel(h_ref, w_ref, wglu_ref, wout_ref, ut_ref, gs_ref, wglu_bf_ref, wout_bf_ref, acc0_ref, acc1_ref, *,
                  n_j, n_u):
    step = pl.program_id(0)
    jp = (step - 1) % n_j
    n_ph, nc, d = h_ref.shape

    def u_epilogue(src):
        for p in range(n_ph):
            ut_ref[p] = src[p * nc:(p + 1) * nc, :].T

    def gate_epilogue(src):
        for p in range(n_ph):
            gs_ref[p] = jax.nn.sigmoid(src[p * nc:(p + 1) * nc, :]).astype(gs_ref.dtype)

    def run_matmul(dst):
        wglu_bf_ref[...] = wglu_ref[...].astype(wglu_bf_ref.dtype)
        wout_bf_ref[...] = wout_ref[...].astype(wout_bf_ref.dtype)
        dst[...] = jnp.dot(h_ref[...].reshape(n_ph * nc, d), w_ref[...], preferred_element_type=F32)

    _overlapped_step(step, run_matmul, [(jp < n_u, u_epilogue), (jp >= n_u, gate_epilogue)], (acc0_ref, acc1_ref))


def _proj_u(h_phase, w_ug, w_glu, w_out, *, ssm_width, d_model, phases=2, tn=512):
    _, nc, d = h_phase.shape
    n_u = ssm_width // tn
    n_j = n_u + d_model // tn
    n_tiles = (CHUNK // phases) * n_j
    cur, prev = _lagged_tiles(n_tiles, n_j)
    n_cast, glu_rows = _side_slabs(n_tiles, w_glu.shape[0])
    _, out_rows = _side_slabs(n_tiles, w_out.shape[0])

    def cast_map(s):
        return jnp.minimum(s, n_cast - 1), 0

    def ut_map(s):
        t, j = prev(s)
        return t, jnp.minimum(j, n_u - 1), 0

    def gs_map(s):
        t, j = prev(s)
        return t, 0, jnp.maximum(j - n_u, 0)

    return pl.pallas_call(
        functools.partial(_proju_kernel, n_j=n_j, n_u=n_u),
        grid=(n_tiles + 1,),
        in_specs=[
            pl.BlockSpec((phases, nc, d), lambda s: (cur(s)[0], 0, 0)),
            pl.BlockSpec((d, tn), lambda s: (0, cur(s)[1])),
            pl.BlockSpec((glu_rows, w_glu.shape[1]), cast_map),
            pl.BlockSpec((out_rows, w_out.shape[1]), cast_map),
        ],
        out_specs=[
            pl.BlockSpec((phases, tn, nc), ut_map),
            pl.BlockSpec((phases, nc, tn), gs_map),
            pl.BlockSpec((glu_rows, w_glu.shape[1]), cast_map),
            pl.BlockSpec((out_rows, w_out.shape[1]), cast_map),
        ],
        out_shape=[jax.ShapeDtypeStruct((CHUNK, ssm_width, nc), F32),
                   jax.ShapeDtypeStruct((CHUNK, nc, d_model), BF16),
                   jax.ShapeDtypeStruct(w_glu.shape, BF16),
                   jax.ShapeDtypeStruct(w_out.shape, BF16)],
        scratch_shapes=[pltpu.VMEM((phases * nc, tn), F32), pltpu.VMEM((phases * nc, tn), F32)],
        compiler_params=_cparams("arbitrary"),
        name="proj_u",
    )(h_phase, w_ug, w_glu, w_out)


def _cmul(ar, ai, br, bi):
    return ar * br - ai * bi, ar * bi + ai * br


def _s5prep_kernel(ldt_ref, lr_ref, li_ref, bre_ref, bim_ref, cre_ref, cim_ref,
                   m_ref, wre_ref, wim_ref, vre_ref, vim_ref, apre_ref, apim_ref, *, n_levels):
    for gi in range(m_ref.shape[0]):
        _s5prep_group(gi, ldt_ref, lr_ref, li_ref, bre_ref, bim_ref, cre_ref, cim_ref,
                      m_ref, wre_ref, wim_ref, vre_ref, vim_ref, apre_ref, apim_ref, n_levels)


def _s5prep_group(gi, ldt_ref, lr_ref, li_ref, bre_ref, bim_ref, cre_ref, cim_ref,
                  m_ref, wre_ref, wim_ref, vre_ref, vim_ref, apre_ref, apim_ref, n_levels):
    p_ = SSM_STATE
    kdim = CHUNK * SSM_GROUP
    first = lax.broadcasted_iota(jnp.int32, (1, 2 * p_), 1) < p_
    own = first if gi % 2 == 0 else jnp.logical_not(first)
    dt = jnp.exp(ldt_ref[gi])
    lr, li = lr_ref[gi], li_ref[gi]
    mag = jnp.exp(lr * dt)
    are, aim = mag * jnp.cos(li * dt), mag * jnp.sin(li * dt)
    den = lr * lr + li * li
    nr = are - 1.0
    coef_re = (nr * lr + aim * li) / den
    coef_im = (aim * lr - nr * li) / den
    bt_re, bt_im = bre_ref[gi], bim_ref[gi]
    bbar_re = coef_re * bt_re - coef_im * bt_im
    bbar_im = coef_re * bt_im + coef_im * bt_re

    pow_re = [jnp.ones_like(are)]
    pow_im = [jnp.zeros_like(are)]
    for _ in range(CHUNK):
        nre, nim = _cmul(pow_re[-1], pow_im[-1], are, aim)
        pow_re.append(nre)
        pow_im.append(nim)

    blocks = []
    for m in range(CHUNK):
        g_re, g_im = _cmul(pow_re[CHUNK - 1 - m], pow_im[CHUNK - 1 - m], bbar_re, bbar_im)
        rows = slice(m * SSM_GROUP, (m + 1) * SSM_GROUP)
        wre_ref[gi, rows, :] = jnp.where(own, g_re, 0.0).astype(wre_ref.dtype)
        wim_ref[gi, rows, :] = jnp.where(own, g_im, 0.0).astype(wim_ref.dtype)
        blocks.append(jnp.where(first, g_re, g_im))
    g_t = jnp.concatenate(blocks, axis=0)

    cre2, cim2 = cre_ref[gi], cim_ref[gi]
    c_mix = jnp.where(first, cre2, -cim2)
    k_rev = lax.dot_general(c_mix, g_t, (((1,), (1,)), ((), ())), precision=lax.Precision.HIGHEST,
                            preferred_element_type=F32)
    k_ext = jnp.concatenate([k_rev, jnp.zeros_like(k_rev)], axis=1)
    for t in range(CHUNK):
        off = (CHUNK - 1 - t) * SSM_GROUP
        m_ref[gi, t * SSM_GROUP:(t + 1) * SSM_GROUP, :] = k_ext[:, off:off + kdim].astype(m_ref.dtype)

    for t in range(CHUNK):
        pr, pi = pow_re[t + 1], pow_im[t + 1]
        rows = slice(t * SSM_GROUP, (t + 1) * SSM_GROUP)
        vre_ref[gi, rows, :] = jnp.where(own, cre2 * pr - cim2 * pi, 0.0).astype(vre_ref.dtype)
        vim_ref[gi, rows, :] = jnp.where(own, -(cre2 * pi + cim2 * pr), 0.0).astype(vim_ref.dtype)

    sub = lax.broadcasted_iota(jnp.int32, (apre_ref.shape[1], 1), 0)
    qr, qi = pow_re[CHUNK], pow_im[CHUNK]
    lev_re = jnp.zeros(apre_ref.shape[1:], F32)
    lev_im = jnp.zeros(apre_ref.shape[1:], F32)
    for k in range(n_levels):
        lev_re = jnp.where((sub == k) & own, qr, lev_re)
        lev_im = jnp.where((sub == k) & own, qi, lev_im)
        qr, qi = _cmul(qr, qi, qr, qi)
    apre_ref[gi] = lev_re
    apim_ref[gi] = lev_im


def _s5_prep(log_dt, lam_re, lam_im, b_re, b_im, c_re, c_im, *, n_levels, gb=4):
    g, p_ = lam_re.shape
    assert gb % 2 == 0 and n_levels <= 2 * SUBLANES

    def dup(a):
        return jnp.concatenate([a, a], axis=-1)

    ldt = log_dt.reshape(g, 1, 1)
    lr, li = dup(lam_re).reshape(g, 1, 2 * p_), dup(lam_im).reshape(g, 1, 2 * p_)
    bt_re, bt_im = dup(jnp.swapaxes(b_re, 1, 2)), dup(jnp.swapaxes(b_im, 1, 2))
    kdim = CHUNK * SSM_GROUP

    def spec(*shape):
        return pl.BlockSpec((gb,) + shape, lambda i: (i, 0, 0))

    op = jax.ShapeDtypeStruct((g, kdim, 2 * p_), BF16)
    lev = jax.ShapeDtypeStruct((g, 2 * SUBLANES, 2 * p_), F32)
    return pl.pallas_call(
        functools.partial(_s5prep_kernel, n_levels=n_levels),
        grid=(g // gb,),
        in_specs=[spec(1, 1), spec(1, 2 * p_), spec(1, 2 * p_), spec(SSM_GROUP, 2 * p_), spec(SSM_GROUP, 2 * p_),
                  spec(SSM_GROUP, 2 * p_), spec(SSM_GROUP, 2 * p_)],
        out_specs=[spec(kdim, kdim)] + [spec(kdim, 2 * p_)] * 4 + [spec(2 * SUBLANES, 2 * p_)] * 2,
        out_shape=[jax.ShapeDtypeStruct((g, kdim, kdim), BF16), op, op, op, op, lev, lev],
        compiler_params=_cparams("arbitrary"),
        name="s5prep",
    )(ldt, lr, li, bt_re, bt_im, dup(c_re), dup(c_im))


def _s5_kernel(x_ref, m_ref, wre_ref, wim_ref, vre_ref, vim_ref, apre_ref, apim_ref, d_ref, o_ref, *, n_levels):
    for pair in range(m_ref.shape[0] // 2):
        _s5_pair(pair, x_ref, m_ref, wre_ref, wim_ref, vre_ref, vim_ref, apre_ref, apim_ref, d_ref, o_ref, n_levels)


def _s5_pair(pair, x_ref, m_ref, wre_ref, wim_ref, vre_ref, vim_ref, apre_ref, apim_ref, d_ref, o_ref, n_levels):
    kdim = CHUNK * SSM_GROUP
    nc = x_ref.shape[-1]
    groups = (2 * pair, 2 * pair + 1)
    chans = [slice(g * SSM_GROUP, (g + 1) * SSM_GROUP) for g in groups]
    xs = [x_ref[:, c, :].reshape(kdim, nc) for c in chans]
    xbs = [x.astype(BF16) for x in xs]
    xcat = jnp.concatenate(xbs, axis=0)
    tn_ = (((0,), (0,)), ((), ()))
    nt = (((1,), (1,)), ((), ()))

    def pair_rows(ref):
        return jnp.concatenate([ref[g] for g in groups], axis=0)

    s_re = lax.dot_general(xcat, pair_rows(wre_ref), tn_, preferred_element_type=F32)
    s_im = lax.dot_general(xcat, pair_rows(wim_ref), tn_, preferred_element_type=F32)
    row = lax.broadcasted_iota(jnp.int32, (nc, 1), 0)

    def shift_rows(v, sh):
        if sh % SUBLANES == 0:
            return jnp.concatenate([jnp.zeros((sh, v.shape[1]), v.dtype), v[:nc - sh]], axis=0)
        return jnp.where(row >= sh, pltpu.roll(v, sh, 0), 0.0)

    lev_re = apre_ref[groups[0]] + apre_ref[groups[1]]
    lev_im = apim_ref[groups[0]] + apim_ref[groups[1]]
    for k in range(n_levels):
        sh = 1 << k
        pr, pi = lev_re[k:k + 1], lev_im[k:k + 1]
        t_re, t_im = shift_rows(s_re, sh), shift_rows(s_im, sh)
        s_re, s_im = s_re + (pr * t_re - pi * t_im), s_im + (pr * t_im + pi * t_re)
    st_re = shift_rows(s_re, 1).astype(BF16)
    st_im = shift_rows(s_im, 1).astype(BF16)

    for g, c, x, xb in zip(groups, chans, xs, xbs):
        y = jnp.dot(m_ref[g], xb, preferred_element_type=F32)
        y = y + lax.dot_general(vre_ref[g], st_re, nt, preferred_element_type=F32)
        y = y + lax.dot_general(vim_ref[g], st_im, nt, preferred_element_type=F32)
        y = y + d_ref[g] * x
        o_ref[:, c, :] = jax.nn.gelu(y).astype(o_ref.dtype).reshape(CHUNK, SSM_GROUP, nc)


def _s5_scan(u_t, ops, d_col, *, n_levels, gb=4):
    _, width, nc = u_t.shape
    g = width // SSM_GROUP
    kdim = CHUNK * SSM_GROUP
    p2 = 2 * SSM_STATE

    def spec(*shape):
        return pl.BlockSpec((gb,) + shape, lambda i: (i, 0, 0))

    io = pl.BlockSpec((CHUNK, gb * SSM_GROUP, nc), lambda i: (0, i, 0))
    return pl.pallas_call(
        functools.partial(_s5_kernel, n_levels=n_levels),
        grid=(g // gb,),
        in_specs=[io, spec(kdim, kdim)] + [spec(kdim, p2)] * 4 + [spec(2 * SUBLANES, p2)] * 2 + [spec(kdim, 1)],
        out_specs=io,
        out_shape=jax.ShapeDtypeStruct((CHUNK, width, nc), BF16),
        compiler_params=_cparams("arbitrary"),
        name="s5",
    )(u_t, *ops, d_col)


def _attn_kernel(sink_ref, q_ref, kc_ref, kp_ref, vc_ref, vp_ref, ga_ref, o_ref, nat_ref, *, n_groups):
    i = pl.program_id(0)
    gh = pl.program_id(1)
    tq = q_ref.shape[0]
    nb = tq // WINDOW
    scale = HEAD_DIM ** -0.5
    row = lax.broadcasted_iota(jnp.int32, (WINDOW, WINDOW), 0)
    col = lax.broadcasted_iota(jnp.int32, (WINDOW, WINDOW), 1)
    tri1 = col <= row
    tri = jnp.concatenate([tri1] * REP, axis=0)
    nt = (((1,), (1,)), ((), ()))
    for b in range(nb):
        rows = slice(b * WINDOW, (b + 1) * WINDOW)
        has_prev = (i * nb + b) > 0
        valid = tri | has_prev
        for g in range(n_groups):
            kcols = slice(g * HEAD_DIM, (g + 1) * HEAD_DIM)
            if b == 0:
                k_prev, v_prev = kp_ref[:, kcols], vp_ref[:, kcols]
            else:
                prows = slice((b - 1) * WINDOW, b * WINDOW)
                k_prev, v_prev = kc_ref[prows, kcols], vc_ref[prows, kcols]
            k_band = jnp.concatenate([k_prev, kc_ref[rows, kcols]], axis=0)
            v_band = jnp.concatenate([v_prev, vc_ref[rows, kcols]], axis=0)
            qcols = [slice((g * REP + r) * HEAD_DIM, (g * REP + r + 1) * HEAD_DIM) for r in range(REP)]
            qg = jnp.concatenate([q_ref[rows, c] for c in qcols], axis=0)
            s2 = lax.dot_general(qg, k_band, nt, preferred_element_type=F32)
            s = jnp.where(tri, s2[:, WINDOW:], s2[:, :WINDOW]) * scale
            s = jnp.where(valid, s, MASK_VALUE)
            probs = []
            for r in range(REP):
                sink = sink_ref[(gh * n_groups + g) * REP + r]
                sr = s[r * WINDOW:(r + 1) * WINDOW]
                m = jnp.maximum(jnp.max(sr, axis=-1, keepdims=True), sink)
                p = jnp.exp(sr - m)
                denom = jnp.sum(p, axis=-1, keepdims=True) + jnp.exp(sink - m)
                probs.append(p * (1.0 / denom))
            pr = jnp.concatenate(probs, axis=0)
            p2 = jnp.concatenate([jnp.where(tri, 0.0, pr), jnp.where(tri, pr, 0.0)], axis=1).astype(BF16)
            o = jnp.dot(p2, v_band, preferred_element_type=F32)
            for r in range(REP):
                gate = ga_ref[rows, qcols[r]].astype(F32)
                nat_ref[rows, qcols[r]] = gate * o[r * WINDOW:(r + 1) * WINDOW]
    grp = CHUNK * CHUNK
    for q in range(tq // grp):
        nat = nat_ref[q * grp:(q + 1) * grp, :].astype(o_ref.dtype)
        o_ref[:, q * CHUNK:(q + 1) * CHUNK, :] = _to_phase_major(nat)


def _attention(sinks, pz, *, d_model, kv_width, tq=256, n_groups=4):
    s = pz.shape[0]
    tq = min(tq, s)
    nb = tq // WINDOW
    wq = n_groups * REP * HEAD_DIM
    wk = n_groups * HEAD_DIM
    k0 = d_model // wk
    v0 = (d_model + kv_width) // wk
    a0 = (d_model + 2 * kv_width) // wq

    def prev_map(col0):
        return lambda i, gh: (jnp.maximum(i * nb - 1, 0), col0 + gh)

    return pl.pallas_call(
        functools.partial(_attn_kernel, n_groups=n_groups),
        grid=(s // tq, d_model // wq),
        in_specs=[
            pl.BlockSpec(memory_space=pltpu.SMEM),
            pl.BlockSpec((tq, wq), lambda i, gh: (i, gh)),
            pl.BlockSpec((tq, wk), lambda i, gh: (i, k0 + gh)),
            pl.BlockSpec((WINDOW, wk), prev_map(k0)),
            pl.BlockSpec((tq, wk), lambda i, gh: (i, v0 + gh)),
            pl.BlockSpec((WINDOW, wk), prev_map(v0)),
            pl.BlockSpec((tq, wq), lambda i, gh: (i, a0 + gh)),
        ],
        out_specs=pl.BlockSpec((CHUNK, tq // CHUNK, wq), lambda i, gh: (0, i, gh)),
        out_shape=jax.ShapeDtypeStruct((CHUNK, s // CHUNK, d_model), BF16),
        scratch_shapes=[pltpu.VMEM((tq, wq), F32)],
        compiler_params=_cparams("arbitrary", "arbitrary"),
        name="attn",
    )(sinks, pz, pz, pz, pz, pz, pz)


def _glu_kernel(y_ref, wa_ref, wb_ref, at_ref, gs_ref, o_ref, yt_ref):
    n_ph, _, nc = y_ref.shape

    @pl.when(pl.program_id(1) == 0)
    def _():
        for p in range(n_ph):
            yt_ref[p * nc:(p + 1) * nc, :] = y_ref[p].T

    y = yt_ref[...]
    a = jnp.dot(y, wa_ref[...], preferred_element_type=F32)
    b = jnp.dot(y, wb_ref[...], preferred_element_type=F32)
    branch = a * jax.nn.sigmoid(b)
    for p in range(n_ph):
        rows = slice(p * nc, (p + 1) * nc)
        o_ref[p] = (at_ref[p].astype(F32) + gs_ref[p].astype(F32) * branch[rows]).astype(o_ref.dtype)


def _glu_merge(y_t, w_glu, attn_phase, gs_phase, *, phases=2, tn=512):
    _, width, nc = y_t.shape
    d_model = attn_phase.shape[-1]
    nj = d_model // tn
    blk = pl.BlockSpec((phases, nc, tn), lambda t, j: (t, 0, j))
    return pl.pallas_call(
        _glu_kernel,
        grid=(CHUNK // phases, nj),
        in_specs=[
            pl.BlockSpec((phases, width, nc), lambda t, j: (t, 0, 0)),
            pl.BlockSpec((width, tn), lambda t, j: (0, j)),
            pl.BlockSpec((width, tn), lambda t, j: (0, nj + j)),
            blk, blk,
        ],
        out_specs=blk,
        out_shape=jax.ShapeDtypeStruct(attn_phase.shape, BF16),
        scratch_shapes=[pltpu.VMEM((phases * nc, width), BF16)],
        compiler_params=_cparams("arbitrary", "arbitrary"),
        name="glu",
    )(y_t, w_glu, w_glu, attn_phase, gs_phase)


def _outp_kernel(a_ref, w_ref, x_ref, g_ref, o_ref, nat_ref):
    grp = CHUNK * CHUNK

    @pl.when(pl.program_id(1) == 0)
    def _():
        for q in range(nat_ref.shape[0] // grp):
            a = jnp.swapaxes(a_ref[:, q * CHUNK:(q + 1) * CHUNK, :], 0, 1)
            nat_ref[q * grp:(q + 1) * grp, :] = a.reshape(grp, a_ref.shape[-1])

    acc = jnp.dot(nat_ref[...], w_ref[...], preferred_element_type=F32)
    o_ref[...] = x_ref[...] + g_ref[...] * acc


def _out_proj(a_phase, w, x, mod, gate_idx, *, tm=1024, tn=512):
    n_ph, nc, k = a_phase.shape
    s, n = x.shape
    tm = min(tm, s)
    return pl.pallas_call(
        _outp_kernel,
        grid=(s // tm, n // tn),
        in_specs=[
            pl.BlockSpec((n_ph, tm // n_ph, k), lambda i, j: (0, i, 0)),
            pl.BlockSpec((k, tn), lambda i, j: (0, j)),
            pl.BlockSpec((tm, tn), lambda i, j: (i, j)),
            pl.BlockSpec((1, tn), lambda i, j: (0, gate_idx * (n // tn) + j)),
        ],
        out_specs=pl.BlockSpec((tm, tn), lambda i, j: (i, j)),
        out_shape=jax.ShapeDtypeStruct((s, n), F32),
        scratch_shapes=[pltpu.VMEM((tm, k), BF16)],
        compiler_params=_cparams("arbitrary", "arbitrary"),
        name="outp",
    )(a_phase, w, x, mod)


def _ff1_kernel(h_ref, w_ref, o_ref):
    acc = jnp.dot(h_ref[...], w_ref[...].astype(BF16), preferred_element_type=F32)
    r = jnp.maximum(acc, 0.0)
    o_ref[...] = (r * r).astype(o_ref.dtype)


def _ff1(h, w, *, tm=1024, tn=512):
    s, k = h.shape
    n = w.shape[1]
    tm = min(tm, s)
    return pl.pallas_call(
        _ff1_kernel,
        grid=(s // tm, n // tn),
        in_specs=[pl.BlockSpec((tm, k), lambda i, j: (i, 0)), pl.BlockSpec((k, tn), lambda i, j: (0, j))],
        out_specs=pl.BlockSpec((tm, tn), lambda i, j: (i, j)),
        out_shape=jax.ShapeDtypeStruct((s, n), BF16),
        compiler_params=_cparams("arbitrary", "arbitrary"),
        name="ff1",
    )(h, w)


def _ff2_kernel(a_ref, w_ref, x_ref, g_ref, o_ref):
    kk = pl.program_id(2)
    last = pl.num_programs(2) - 1

    def product():
        return jnp.dot(a_ref[...], w_ref[...].astype(BF16), preferred_element_type=F32)

    @pl.when(kk == 0)
    def _():
        o_ref[...] = product()

    @pl.when((kk > 0) & (kk < last))
    def _():
        o_ref[...] += product()

    @pl.when(kk == last)
    def _():
        o_ref[...] = x_ref[...] + g_ref[...] * (o_ref[...] + product())


def _ff2(a, w, x, mod, gate_idx, *, tm=2048, tn=1024, tk=1024):
    s, k = a.shape
    n = w.shape[1]
    tm = min(tm, s)
    assert k // tk >= 2
    return pl.pallas_call(
        _ff2_kernel,
        grid=(s // tm, n // tn, k // tk),
        in_specs=[
            pl.BlockSpec((tm, tk), lambda i, j, kk: (i, kk)),
            pl.BlockSpec((tk, tn), lambda i, j, kk: (kk, j)),
            pl.BlockSpec((tm, tn), lambda i, j, kk: (i, j)),
            pl.BlockSpec((1, tn), lambda i, j, kk: (0, gate_idx * (n // tn) + j)),
        ],
        out_specs=pl.BlockSpec((tm, tn), lambda i, j, kk: (i, j)),
        out_shape=jax.ShapeDtypeStruct((s, n), F32),
        compiler_params=_cparams("arbitrary", "arbitrary", "arbitrary"),
        name="ff2",
    )(a, w, x, mod)


def _layer(x, c_col, pos_col, w_ada, b_ada, norm1_g, norm2_g, w_in, q_norm_g, k_norm_g, attn_sinks,
           lam_re, lam_im, log_dt, b_re, b_im, c_re, c_im, ssm_d, w_glu, w_out, w_ff1, w_ff2):
    s, d = x.shape
    kv_width = N_KV_HEADS * HEAD_DIM
    ssm_width = ssm_d.shape[0]
    n_groups = ssm_width // SSM_GROUP
    nc = s // CHUNK
    n_levels = max(1, (nc - 1).bit_length())

    b_row = b_ada.reshape(1, -1)
    mod_a, c_bcast = _modulation(c_col, w_ada, b_row, 2 * d)
    h, h_phase = _norm_mod(x, norm1_g.reshape(1, d), mod_a, 0, with_phase_major=True)
    rope_c, rope_sa, rope_sb = _rope_tables(pos_col)
    pz, mod_b, w_ug = _proj(h, w_in, q_norm_g.reshape(1, -1), k_norm_g.reshape(1, -1), rope_c, rope_sa, rope_sb,
                            c_bcast, w_ada, b_row, 2 * d, d_model=d, kv_width=kv_width, ssm_width=ssm_width)
    u_t, gs_phase, w_glu_bf, w_out_bf = _proj_u(h_phase, w_ug, w_glu, w_out, ssm_width=ssm_width, d_model=d)

    s5_ops = _s5_prep(log_dt, lam_re, lam_im, b_re, b_im, c_re, c_im, n_levels=n_levels)
    d_col = jnp.tile(ssm_d.reshape(n_groups, 1, SSM_GROUP), (1, CHUNK, 1)).reshape(n_groups, CHUNK * SSM_GROUP, 1)
    y_t = _s5_scan(u_t, s5_ops, d_col, n_levels=n_levels)

    attn_phase = _attention(attn_sinks, pz, d_model=d, kv_width=kv_width)
    merged_phase = _glu_merge(y_t, w_glu_bf, attn_phase, gs_phase)
    x1 = _out_proj(merged_phase, w_out_bf, x, mod_b, 0)

    (h2,) = _norm_mod(x1, norm2_g.reshape(1, d), mod_b, 1)
    act = _ff1(h2, w_ff1)
    return _ff2(act, w_ff2, x1, mod_b, 3)


def kernel(x, c, positions, w_ada, b_ada, norm1_g, norm2_g, w_in, q_norm_g, k_norm_g, attn_sinks, ssm_lam_re,
           ssm_lam_im, ssm_log_dt, ssm_b_re, ssm_b_im, ssm_c_re, ssm_c_im, ssm_d, w_glu, w_out, w_ff1, w_ff2):
    bsz, s, d = x.shape
    assert bsz == 1 and d == N_Q_HEADS * HEAD_DIM and s % (CHUNK * LANES) == 0
    xs = x[0]
    c_col = c.reshape(d, 1)
    pos_col = positions.reshape(s, 1)
    for l in range(w_ada.shape[0]):
        xs = _layer(xs, c_col, pos_col, w_ada[l], b_ada[l], norm1_g[l], norm2_g[l], w_in[l], q_norm_g[l],
                    k_norm_g[l], attn_sinks[l], ssm_lam_re[l], ssm_lam_im[l], ssm_log_dt[l], ssm_b_re[l],
                    ssm_b_im[l], ssm_c_re[l], ssm_c_im[l], ssm_d[l], w_glu[l], w_out[l], w_ff1[l], w_ff2[l])
    return xs[None]
```

```python
import functools
import math

import jax
import jax.numpy as jnp
from jax import lax
from jax.experimental import pallas as pl
from jax.experimental.pallas import tpu as pltpu

F32 = jnp.float32
BF16 = jnp.bfloat16

LANES = 128
SUBLANES = 8
MXU_DIM = 256
VMEM_LIMIT_BYTES = 56 * 1024 * 1024

N_Q_HEADS = 32
N_KV_HEADS = 8
HEAD_DIM = 128
REP = N_Q_HEADS // N_KV_HEADS
WINDOW = 128
ROPE_DIM = HEAD_DIM // 4
ROPE_HALF = ROPE_DIM // 2
ROPE_THETA = 500000.0
SSM_GROUP = 16
SSM_STATE = 64
N_MOD = 6
EPS = 1e-6
MASK_VALUE = -1e30
CHUNK = MXU_DIM // SSM_GROUP


def _phase_perm():
    n = CHUNK * CHUNK
    r = lax.broadcasted_iota(jnp.int32, (n, n), 0)
    c = lax.broadcasted_iota(jnp.int32, (n, n), 1)
    return jnp.where((r % CHUNK) * CHUNK + r // CHUNK == c, 1.0, 0.0).astype(BF16)


def _to_phase_major(v, perm):
    return jnp.dot(perm, v, preferred_element_type=F32).astype(v.dtype).reshape(CHUNK, CHUNK, v.shape[-1])


def _lagged_tiles(n_tiles, n_inner):
    def cur(step):
        t = jnp.minimum(step, n_tiles - 1)
        return t // n_inner, t % n_inner

    def prev(step):
        t = jnp.maximum(step - 1, 0)
        return t // n_inner, t % n_inner

    return cur, prev


def _overlapped_step(step, run_matmul, epilogues, accs):
    cases = [(step == 0, None)] + [((step > 0) & cond, fn) for cond, fn in epilogues]
    for parity in (0, 1):
        for cond, fn in cases:
            @pl.when(cond & (step % 2 == parity))
            def _(dst=accs[parity], src=accs[1 - parity], fn=fn):
                if fn is not None:
                    fn(src)
                run_matmul(dst)


def _side_slabs(n_tiles, rows):
    n_side = 1 << (n_tiles.bit_length() - 1)
    assert rows % (n_side * 2 * SUBLANES) == 0
    return n_side, rows // n_side


def _cparams(*sem):
    return pltpu.CompilerParams(dimension_semantics=sem, vmem_limit_bytes=VMEM_LIMIT_BYTES)


def _mod_kernel(c_ref, w_ref, b_ref, o_ref, cb_ref, *, rk):
    d, tn = w_ref.shape

    def body(r, acc):
        rows = pl.ds(pl.multiple_of(r * rk, rk), rk)
        c = c_ref[rows, :]
        ca = c * jax.nn.sigmoid(c)
        cb_ref[rows, :] = jnp.broadcast_to(ca, (rk, LANES))
        prod = w_ref[rows, :] * ca
        return acc + jnp.sum(prod.reshape(rk // SUBLANES, SUBLANES, tn), axis=0)

    acc = lax.fori_loop(0, d // rk, body, jnp.zeros((SUBLANES, tn), F32))
    o_ref[...] = jnp.sum(acc, axis=0, keepdims=True) + b_ref[...]


def _modulation(c_col, w_ada, b_ada_row, n_cols, *, tn=512, rk=512):
    d = w_ada.shape[0]
    return pl.pallas_call(
        functools.partial(_mod_kernel, rk=rk),
        grid=(n_cols // tn,),
        in_specs=[
            pl.BlockSpec((d, 1), lambda j: (0, 0)),
            pl.BlockSpec((d, tn), lambda j: (0, j)),
            pl.BlockSpec((1, tn), lambda j: (0, j)),
        ],
        out_specs=[pl.BlockSpec((1, tn), lambda j: (0, j)), pl.BlockSpec((d, LANES), lambda j: (0, 0))],
        out_shape=[jax.ShapeDtypeStruct((1, n_cols), F32), jax.ShapeDtypeStruct((d, LANES), F32)],
        compiler_params=_cparams("arbitrary"),
        name="mod",
    )(c_col, w_ada, b_ada_row)


def _mod_side_task(cb_ref, wm_ref, bm_ref, om_ref):
    d, wc = wm_ref.shape
    cb = cb_ref[...]
    for q in range(wc // LANES):
        cols = slice(q * LANES, (q + 1) * LANES)
        part = jnp.sum((wm_ref[:, cols] * cb).reshape(d // SUBLANES, SUBLANES, LANES), axis=0)
        om_ref[:, cols] = jnp.sum(part, axis=0, keepdims=True) + bm_ref[:, cols]


def _norm_kernel(x_ref, g_ref, sh_ref, sc_ref, o_ref, *phase_ref):
    def normed(x):
        ms = jnp.mean(x * x, axis=-1, keepdims=True)
        y = x * lax.rsqrt(ms + EPS) * g_ref[...]
        return (y * (1.0 + sc_ref[...]) + sh_ref[...]).astype(o_ref.dtype)

    hb = normed(x_ref[...])
    o_ref[...] = hb
    for oph_ref in phase_ref:
        perm = _phase_perm()
        grp = CHUNK * CHUNK
        for q in range(hb.shape[0] // grp):
            oph_ref[:, q * CHUNK:(q + 1) * CHUNK, :] = _to_phase_major(hb[q * grp:(q + 1) * grp], perm)


def _norm_mod(x, g_row, mod, shift_idx, *, with_phase_major=False, tm=512):
    s, d = x.shape
    out_specs = [pl.BlockSpec((tm, d), lambda i: (i, 0))]
    out_shape = [jax.ShapeDtypeStruct((s, d), BF16)]
    if with_phase_major:
        out_specs.append(pl.BlockSpec((CHUNK, tm // CHUNK, d), lambda i: (0, i, 0)))
        out_shape.append(jax.ShapeDtypeStruct((CHUNK, s // CHUNK, d), BF16))
    return pl.pallas_call(
        _norm_kernel,
        grid=(s // tm,),
        in_specs=[
            pl.BlockSpec((tm, d), lambda i: (i, 0)),
            pl.BlockSpec((1, d), lambda i: (0, 0)),
            pl.BlockSpec((1, d), lambda i: (0, shift_idx)),
            pl.BlockSpec((1, d), lambda i: (0, shift_idx + 1)),
        ],
        out_specs=out_specs,
        out_shape=out_shape,
        compiler_params=_cparams("arbitrary"),
        name="norm",
    )(x, g_row, mod, mod)


def _rope_kernel(pos_ref, c_ref, sa_ref, sb_ref):
    pos = pos_ref[...].astype(F32)
    lane = lax.broadcasted_iota(jnp.int32, (1, LANES), 1)
    idx = (lane & (ROPE_HALF - 1)).astype(F32)
    inv_freq = jnp.power(jnp.float32(ROPE_THETA), -2.0 * idx / ROPE_DIM)
    ang = pos * inv_freq
    cos = jnp.cos(ang)
    sin = jnp.sin(ang)
    c_ref[...] = jnp.where(lane < ROPE_DIM, cos, 1.0)
    sa_ref[...] = jnp.where(lane < ROPE_HALF, -sin, 0.0)
    sb_ref[...] = jnp.where((lane >= ROPE_HALF) & (lane < ROPE_DIM), sin, 0.0)


def _rope_tables(pos_col, *, tm=1024):
    s = pos_col.shape[0]
    tm = min(tm, s)
    spec = pl.BlockSpec((tm, LANES), lambda i: (i, 0))
    shp = jax.ShapeDtypeStruct((s, LANES), F32)
    return pl.pallas_call(
        _rope_kernel,
        grid=(s // tm,),
        in_specs=[pl.BlockSpec((tm, 1), lambda i: (i, 0))],
        out_specs=[spec, spec, spec],
        out_shape=[shp, shp, shp],
        compiler_params=_cparams("arbitrary"),
        name="rope",
    )(pos_col)


def _proj_kernel(h_ref, w_ref, qg_ref, kg_ref, rc_ref, rsa_ref, rsb_ref, cb_ref, wm_ref, bm_ref, o_ref, om_ref,
                 acc0_ref, acc1_ref, *, n_j, n_q, n_qk, n_qkv):
    step = pl.program_id(0)
    jp = (step - 1) % n_j
    tn = o_ref.shape[1]

    def qk_epilogue(src):
        g = jnp.where(jp < n_q, qg_ref[...], kg_ref[...])
        c, sa, sb = rc_ref[...], rsa_ref[...], rsb_ref[...]
        for hh in range(tn // HEAD_DIM):
            cols = slice(hh * HEAD_DIM, (hh + 1) * HEAD_DIM)
            xh = src[:, cols]
            ms = jnp.mean(xh * xh, axis=-1, keepdims=True)
            y = xh * lax.rsqrt(ms + EPS) * g
            y = y * c + pltpu.roll(y, HEAD_DIM - ROPE_HALF, 1) * sa + pltpu.roll(y, ROPE_HALF, 1) * sb
            o_ref[:, cols] = y.astype(o_ref.dtype)

    def v_epilogue(src):
        o_ref[...] = src[...].astype(o_ref.dtype)

    def gate_epilogue(src):
        o_ref[...] = jax.nn.sigmoid(src[...]).astype(o_ref.dtype)

    def run_matmul(dst):
        _mod_side_task(cb_ref, wm_ref, bm_ref, om_ref)
        dst[...] = jnp.dot(h_ref[...], w_ref[...].astype(BF16), preferred_element_type=F32)

    _overlapped_step(step, run_matmul,
                     [(jp < n_qk, qk_epilogue), ((jp >= n_qk) & (jp < n_qkv), v_epilogue), (jp >= n_qkv, gate_epilogue)],
                     (acc0_ref, acc1_ref))


def _proj(h, w_in, qg_row, kg_row, rope_c, rope_sa, rope_sb, c_bcast, w_ada, b_ada_row, mod_col0, *,
          d_model, kv_width, ssm_width, tm=1024, tn=512):
    s, d = h.shape
    tm = min(tm, s)
    n_q = d_model // tn
    n_qk = n_q + kv_width // tn
    n_qkv = n_qk + kv_width // tn
    n_u = ssm_width // tn
    n_j = n_qkv + d_model // tn
    n_tiles = (s // tm) * n_j
    cur, prev = _lagged_tiles(n_tiles, n_j)

    def w_map(t):
        j = cur(t)[1]
        return 0, jnp.where(j < n_qkv, j, j + n_u)

    n_mod = w_ada.shape[1] - mod_col0
    wc = LANES * pl.cdiv(n_mod // LANES, n_tiles)
    n_side = n_mod // wc
    assert n_mod % wc == 0 and mod_col0 % wc == 0

    def side(t):
        return jnp.minimum(t, n_side - 1)

    tab = pl.BlockSpec((tm, LANES), lambda t: (prev(t)[0], 0))
    row = pl.BlockSpec((1, HEAD_DIM), lambda t: (0, 0))
    return pl.pallas_call(
        functools.partial(_proj_kernel, n_j=n_j, n_q=n_q, n_qk=n_qk, n_qkv=n_qkv),
        grid=(n_tiles + 1,),
        in_specs=[
            pl.BlockSpec((tm, d), lambda t: (cur(t)[0], 0)),
            pl.BlockSpec((d, tn), w_map),
            row, row, tab, tab, tab,
            pl.BlockSpec((d, LANES), lambda t: (0, 0)),
            pl.BlockSpec((d, wc), lambda t: (0, mod_col0 // wc + side(t))),
            pl.BlockSpec((1, wc), lambda t: (0, mod_col0 // wc + side(t))),
        ],
        out_specs=[pl.BlockSpec((tm, tn), lambda t: prev(t)), pl.BlockSpec((1, wc), lambda t: (0, side(t)))],
        out_shape=[jax.ShapeDtypeStruct((s, n_j * tn), BF16), jax.ShapeDtypeStruct((1, n_mod), F32)],
        scratch_shapes=[pltpu.VMEM((tm, tn), F32), pltpu.VMEM((tm, tn), F32)],
        compiler_params=_cparams("arbitrary"),
        name="proj",
    )(h, w_in, qg_row, kg_row, rope_c, rope_sa, rope_sb, c_bcast, w_ada, b_ada_row)


def _proju_kernel(h_ref, w_ref, wglu_ref, wout_ref, ut_ref, gs_ref, wglu_bf_ref, wout_bf_ref, acc0_ref, acc1_ref, *,
                  n_j, n_u):
    step = pl.program_id(0)
    jp = (step - 1) % n_j
    n_ph, nc, d = h_ref.shape

    def u_epilogue(src):
        for p in range(n_ph):
            ut_ref[p] = src[p * nc:(p + 1) * nc, :].T

    def gate_epilogue(src):
        for p in range(n_ph):
            gs_ref[p] = jax.nn.sigmoid(src[p * nc:(p + 1) * nc, :]).astype(gs_ref.dtype)

    def run_matmul(dst):
        wglu_bf_ref[...] = wglu_ref[...].astype(wglu_bf_ref.dtype)
        wout_bf_ref[...] = wout_ref[...].astype(wout_bf_ref.dtype)
        dst[...] = jnp.dot(h_ref[...].reshape(n_ph * nc, d), w_ref[...].astype(BF16), preferred_element_type=F32)

    _overlapped_step(step, run_matmul, [(jp < n_u, u_epilogue), (jp >= n_u, gate_epilogue)], (acc0_ref, acc1_ref))


def _proj_u(h_phase, w_in, w_glu, w_out, *, u_col0, gs_col0, ssm_width, d_model, phases=2, tn=512):
    _, nc, d = h_phase.shape
    n_u = ssm_width // tn
    n_j = n_u + d_model // tn
    n_tiles = (CHUNK // phases) * n_j
    cur, prev = _lagged_tiles(n_tiles, n_j)
    n_cast, glu_rows = _side_slabs(n_tiles, w_glu.shape[0])
    _, out_rows = _side_slabs(n_tiles, w_out.shape[0])

    def cast_map(s):
        return jnp.minimum(s, n_cast - 1), 0

    def w_map(s):
        j = cur(s)[1]
        return 0, jnp.where(j < n_u, u_col0 // tn + j, gs_col0 // tn + j - n_u)

    def ut_map(s):
        t, j = prev(s)
        return t, jnp.minimum(j, n_u - 1), 0

    def gs_map(s):
        t, j = prev(s)
        return t, 0, jnp.maximum(j - n_u, 0)

    return pl.pallas_call(
        functools.partial(_proju_kernel, n_j=n_j, n_u=n_u),
        grid=(n_tiles + 1,),
        in_specs=[
            pl.BlockSpec((phases, nc, d), lambda s: (cur(s)[0], 0, 0)),
            pl.BlockSpec((d, tn), w_map),
            pl.BlockSpec((glu_rows, w_glu.shape[1]), cast_map),
            pl.BlockSpec((out_rows, w_out.shape[1]), cast_map),
        ],
        out_specs=[
            pl.BlockSpec((phases, tn, nc), ut_map),
            pl.BlockSpec((phases, nc, tn), gs_map),
            pl.BlockSpec((glu_rows, w_glu.shape[1]), cast_map),
            pl.BlockSpec((out_rows, w_out.shape[1]), cast_map),
        ],
        out_shape=[jax.ShapeDtypeStruct((CHUNK, ssm_width, nc), F32),
                   jax.ShapeDtypeStruct((CHUNK, nc, d_model), BF16),
                   jax.ShapeDtypeStruct(w_glu.shape, BF16),
                   jax.ShapeDtypeStruct(w_out.shape, BF16)],
        scratch_shapes=[pltpu.VMEM((phases * nc, tn), F32), pltpu.VMEM((phases * nc, tn), F32)],
        compiler_params=_cparams("arbitrary"),
        name="proj_u",
    )(h_phase, w_in, w_glu, w_out)


def _cmul(ar, ai, br, bi):
    return ar * br - ai * bi, ar * bi + ai * br


def _s5prep_kernel(ldt_ref, lr_ref, li_ref, bre_ref, bim_ref, cre_ref, cim_ref,
                   m_ref, wre_ref, wim_ref, vre_ref, vim_ref, apre_ref, apim_ref, *, n_levels):
    for gi in range(m_ref.shape[0]):
        _s5prep_group(gi, ldt_ref, lr_ref, li_ref, bre_ref, bim_ref, cre_ref, cim_ref,
                      m_ref, wre_ref, wim_ref, vre_ref, vim_ref, apre_ref, apim_ref, n_levels)


def _s5prep_group(gi, ldt_ref, lr_ref, li_ref, bre_ref, bim_ref, cre_ref, cim_ref,
                  m_ref, wre_ref, wim_ref, vre_ref, vim_ref, apre_ref, apim_ref, n_levels):
    p_ = SSM_STATE
    kdim = CHUNK * SSM_GROUP
    first = lax.broadcasted_iota(jnp.int32, (1, 2 * p_), 1) < p_
    own = first if gi % 2 == 0 else jnp.logical_not(first)
    dt = jnp.exp(ldt_ref[gi])
    lr, li = lr_ref[gi], li_ref[gi]
    mag = jnp.exp(lr * dt)
    are, aim = mag * jnp.cos(li * dt), mag * jnp.sin(li * dt)
    den = lr * lr + li * li
    nr = are - 1.0
    coef_re = (nr * lr + aim * li) / den
    coef_im = (aim * lr - nr * li) / den
    bt_re, bt_im = bre_ref[gi], bim_ref[gi]
    bbar_re = coef_re * bt_re - coef_im * bt_im
    bbar_im = coef_re * bt_im + coef_im * bt_re

    pow_re = [jnp.ones_like(are)]
    pow_im = [jnp.zeros_like(are)]
    for _ in range(CHUNK):
        nre, nim = _cmul(pow_re[-1], pow_im[-1], are, aim)
        pow_re.append(nre)
        pow_im.append(nim)

    blocks = []
    for m in range(CHUNK):
        g_re, g_im = _cmul(pow_re[CHUNK - 1 - m], pow_im[CHUNK - 1 - m], bbar_re, bbar_im)
        rows = slice(m * SSM_GROUP, (m + 1) * SSM_GROUP)
        wre_ref[gi, rows, :] = jnp.where(own, g_re, 0.0).astype(wre_ref.dtype)
        wim_ref[gi, rows, :] = jnp.where(own, g_im, 0.0).astype(wim_ref.dtype)
        blocks.append(jnp.where(first, g_re, g_im))
    g_t = jnp.concatenate(blocks, axis=0)

    cre2, cim2 = cre_ref[gi], cim_ref[gi]
    c_mix = jnp.where(first, cre2, -cim2)
    k_rev = lax.dot_general(c_mix, g_t, (((1,), (1,)), ((), ())), precision=lax.Precision.HIGHEST,
                            preferred_element_type=F32)
    k_ext = jnp.concatenate([k_rev, jnp.zeros_like(k_rev)], axis=1)
    for t in range(CHUNK):
        off = (CHUNK - 1 - t) * SSM_GROUP
        m_ref[gi, t * SSM_GROUP:(t + 1) * SSM_GROUP, :] = k_ext[:, off:off + kdim].astype(m_ref.dtype)

    for t in range(CHUNK):
        pr, pi = pow_re[t + 1], pow_im[t + 1]
        rows = slice(t * SSM_GROUP, (t + 1) * SSM_GROUP)
        vre_ref[gi, rows, :] = jnp.where(own, cre2 * pr - cim2 * pi, 0.0).astype(vre_ref.dtype)
        vim_ref[gi, rows, :] = jnp.where(own, -(cre2 * pi + cim2 * pr), 0.0).astype(vim_ref.dtype)

    sub = lax.broadcasted_iota(jnp.int32, (apre_ref.shape[1], 1), 0)
    qr, qi = pow_re[CHUNK], pow_im[CHUNK]
    lev_re = jnp.zeros(apre_ref.shape[1:], F32)
    lev_im = jnp.zeros(apre_ref.shape[1:], F32)
    for k in range(n_levels):
        lev_re = jnp.where((sub == k) & own, qr, lev_re)
        lev_im = jnp.where((sub == k) & own, qi, lev_im)
        qr, qi = _cmul(qr, qi, qr, qi)
    apre_ref[gi] = lev_re
    apim_ref[gi] = lev_im


def _s5_prep(log_dt, lam_re, lam_im, b_re, b_im, c_re, c_im, *, n_levels, gb=4):
    g, p_ = lam_re.shape
    assert gb % 2 == 0 and n_levels <= 2 * SUBLANES

    def dup(a):
        return jnp.concatenate([a, a], axis=-1)

    ldt = log_dt.reshape(g, 1, 1)
    lr, li = dup(lam_re).reshape(g, 1, 2 * p_), dup(lam_im).reshape(g, 1, 2 * p_)
    bt_re, bt_im = dup(jnp.swapaxes(b_re, 1, 2)), dup(jnp.swapaxes(b_im, 1, 2))
    kdim = CHUNK * SSM_GROUP

    def spec(*shape):
        return pl.BlockSpec((gb,) + shape, lambda i: (i, 0, 0))

    op = jax.ShapeDtypeStruct((g, kdim, 2 * p_), BF16)
    lev = jax.ShapeDtypeStruct((g, 2 * SUBLANES, 2 * p_), F32)
    return pl.pallas_call(
        functools.partial(_s5prep_kernel, n_levels=n_levels),
        grid=(g // gb,),
        in_specs=[spec(1, 1), spec(1, 2 * p_), spec(1, 2 * p_), spec(SSM_GROUP, 2 * p_), spec(SSM_GROUP, 2 * p_),
                  spec(SSM_GROUP, 2 * p_), spec(SSM_GROUP, 2 * p_)],
        out_specs=[spec(kdim, kdim)] + [spec(kdim, 2 * p_)] * 4 + [spec(2 * SUBLANES, 2 * p_)] * 2,
        out_shape=[jax.ShapeDtypeStruct((g, kdim, kdim), BF16), op, op, op, op, lev, lev],
        compiler_params=_cparams("arbitrary"),
        name="s5prep",
    )(ldt, lr, li, bt_re, bt_im, dup(c_re), dup(c_im))


def _s5_kernel(x_ref, m_ref, wre_ref, wim_ref, vre_ref, vim_ref, apre_ref, apim_ref, d_ref, o_ref, *, n_levels):
    for pair in range(m_ref.shape[0] // 2):
        _s5_pair(pair, x_ref, m_ref, wre_ref, wim_ref, vre_ref, vim_ref, apre_ref, apim_ref, d_ref, o_ref, n_levels)


def _s5_pair(pair, x_ref, m_ref, wre_ref, wim_ref, vre_ref, vim_ref, apre_ref, apim_ref, d_ref, o_ref, n_levels):
    kdim = CHUNK * SSM_GROUP
    nc = x_ref.shape[-1]
    groups = (2 * pair, 2 * pair + 1)
    chans = [slice(g * SSM_GROUP, (g + 1) * SSM_GROUP) for g in groups]
    xs = [x_ref[:, c, :].reshape(kdim, nc) for c in chans]
    xbs = [x.astype(BF16) for x in xs]
    xcat = jnp.concatenate(xbs, axis=0)
    tn_ = (((0,), (0,)), ((), ()))
    nt = (((1,), (1,)), ((), ()))

    def pair_rows(ref):
        return jnp.concatenate([ref[g] for g in groups], axis=0)

    s_re = lax.dot_general(xcat, pair_rows(wre_ref), tn_, preferred_element_type=F32)
    s_im = lax.dot_general(xcat, pair_rows(wim_ref), tn_, preferred_element_type=F32)
    row = lax.broadcasted_iota(jnp.int32, (nc, 1), 0)

    def shift_rows(v, sh):
        if sh % SUBLANES == 0:
            return jnp.concatenate([jnp.zeros((sh, v.shape[1]), v.dtype), v[:nc - sh]], axis=0)
        return jnp.where(row >= sh, pltpu.roll(v, sh, 0), 0.0)

    lev_re = apre_ref[groups[0]] + apre_ref[groups[1]]
    lev_im = apim_ref[groups[0]] + apim_ref[groups[1]]
    for k in range(n_levels):
        sh = 1 << k
        pr, pi = lev_re[k:k + 1], lev_im[k:k + 1]
        t_re, t_im = shift_rows(s_re, sh), shift_rows(s_im, sh)
        s_re, s_im = s_re + (pr * t_re - pi * t_im), s_im + (pr * t_im + pi * t_re)
    st_re = shift_rows(s_re, 1).astype(BF16)
    st_im = shift_rows(s_im, 1).astype(BF16)

    for g, c, x, xb in zip(groups, chans, xs, xbs):
        y = jnp.dot(m_ref[g], xb, preferred_element_type=F32)
        y = y + lax.dot_general(vre_ref[g], st_re, nt, preferred_element_type=F32)
        y = y + lax.dot_general(vim_ref[g], st_im, nt, preferred_element_type=F32)
        y = y + d_ref[g] * x
        o_ref[:, c, :] = jax.nn.gelu(y).astype(o_ref.dtype).reshape(CHUNK, SSM_GROUP, nc)


def _s5_scan(u_t, ops, d_col, *, n_levels, gb=4):
    _, width, nc = u_t.shape
    g = width // SSM_GROUP
    kdim = CHUNK * SSM_GROUP
    p2 = 2 * SSM_STATE

    def spec(*shape):
        return pl.BlockSpec((gb,) + shape, lambda i: (i, 0, 0))

    io = pl.BlockSpec((CHUNK, gb * SSM_GROUP, nc), lambda i: (0, i, 0))
    return pl.pallas_call(
        functools.partial(_s5_kernel, n_levels=n_levels),
        grid=(g // gb,),
        in_specs=[io, spec(kdim, kdim)] + [spec(kdim, p2)] * 4 + [spec(2 * SUBLANES, p2)] * 2 + [spec(kdim, 1)],
        out_specs=io,
        out_shape=jax.ShapeDtypeStruct((CHUNK, width, nc), BF16),
        compiler_params=_cparams("arbitrary"),
        name="s5",
    )(u_t, *ops, d_col)


def _attn_kernel(sink_ref, q_ref, kc_ref, kp_ref, vc_ref, vp_ref, ga_ref, o_ref, nat_ref, *, n_groups):
    i = pl.program_id(0)
    gh = pl.program_id(1)
    tq = q_ref.shape[0]
    nb = tq // WINDOW
    scale = HEAD_DIM ** -0.5
    row = lax.broadcasted_iota(jnp.int32, (WINDOW, WINDOW), 0)
    col = lax.broadcasted_iota(jnp.int32, (WINDOW, WINDOW), 1)
    tri1 = col <= row
    tri = jnp.concatenate([tri1] * REP, axis=0)
    nt = (((1,), (1,)), ((), ()))
    for b in range(nb):
        rows = slice(b * WINDOW, (b + 1) * WINDOW)
        has_prev = (i * nb + b) > 0
        valid = tri | has_prev
        for g in range(n_groups):
            kcols = slice(g * HEAD_DIM, (g + 1) * HEAD_DIM)
            if b == 0:
                k_prev, v_prev = kp_ref[:, kcols], vp_ref[:, kcols]
            else:
                prows = slice((b - 1) * WINDOW, b * WINDOW)
                k_prev, v_prev = kc_ref[prows, kcols], vc_ref[prows, kcols]
            k_band = jnp.concatenate([k_prev, kc_ref[rows, kcols]], axis=0)
            v_band = jnp.concatenate([v_prev, vc_ref[rows, kcols]], axis=0)
            qcols = [slice((g * REP + r) * HEAD_DIM, (g * REP + r + 1) * HEAD_DIM) for r in range(REP)]
            qg = jnp.concatenate([q_ref[rows, c] for c in qcols], axis=0)
            s2 = lax.dot_general(qg, k_band, nt, preferred_element_type=F32)
            s = jnp.where(tri, s2[:, WINDOW:], s2[:, :WINDOW]) * scale
            s = jnp.where(valid, s, MASK_VALUE)
            probs = []
            for r in range(REP):
                sink = sink_ref[(gh * n_groups + g) * REP + r]
                sr = s[r * WINDOW:(r + 1) * WINDOW]
                m = jnp.maximum(jnp.max(sr, axis=-1, keepdims=True), sink)
                p = jnp.exp(sr - m)
                denom = jnp.sum(p, axis=-1, keepdims=True) + jnp.exp(sink - m)
                probs.append(p * (1.0 / denom))
            pr = jnp.concatenate(probs, axis=0)
            p2 = jnp.concatenate([jnp.where(tri, 0.0, pr), jnp.where(tri, pr, 0.0)], axis=1).astype(BF16)
            o = jnp.dot(p2, v_band, preferred_element_type=F32)
            for r in range(REP):
                gate = ga_ref[rows, qcols[r]].astype(F32)
                nat_ref[rows, qcols[r]] = gate * o[r * WINDOW:(r + 1) * WINDOW]
    perm = _phase_perm()
    grp = CHUNK * CHUNK
    for q in range(tq // grp):
        nat = nat_ref[q * grp:(q + 1) * grp, :].astype(o_ref.dtype)
        o_ref[:, q * CHUNK:(q + 1) * CHUNK, :] = _to_phase_major(nat, perm)


def _attention(sinks, pz, *, d_model, kv_width, tq=256, n_groups=4):
    s = pz.shape[0]
    tq = min(tq, s)
    nb = tq // WINDOW
    wq = n_groups * REP * HEAD_DIM
    wk = n_groups * HEAD_DIM
    k0 = d_model // wk
    v0 = (d_model + kv_width) // wk
    a0 = (d_model + 2 * kv_width) // wq

    def prev_map(col0):
        return lambda i, gh: (jnp.maximum(i * nb - 1, 0), col0 + gh)

    return pl.pallas_call(
        functools.partial(_attn_kernel, n_groups=n_groups),
        grid=(s // tq, d_model // wq),
        in_specs=[
            pl.BlockSpec(memory_space=pltpu.SMEM),
            pl.BlockSpec((tq, wq), lambda i, gh: (i, gh)),
            pl.BlockSpec((tq, wk), lambda i, gh: (i, k0 + gh)),
            pl.BlockSpec((WINDOW, wk), prev_map(k0)),
            pl.BlockSpec((tq, wk), lambda i, gh: (i, v0 + gh)),
            pl.BlockSpec((WINDOW, wk), prev_map(v0)),
            pl.BlockSpec((tq, wq), lambda i, gh: (i, a0 + gh)),
        ],
        out_specs=pl.BlockSpec((CHUNK, tq // CHUNK, wq), lambda i, gh: (0, i, gh)),
        out_shape=jax.ShapeDtypeStruct((CHUNK, s // CHUNK, d_model), BF16),
        scratch_shapes=[pltpu.VMEM((tq, wq), F32)],
        compiler_params=_cparams("arbitrary", "arbitrary"),
        name="attn",
    )(sinks, pz, pz, pz, pz, pz, pz)


def _glu_kernel(y_ref, wa_ref, wb_ref, at_ref, gs_ref, o_ref, yt_ref):
    n_ph, _, nc = y_ref.shape

    @pl.when(pl.program_id(1) == 0)
    def _():
        for p in range(n_ph):
            yt_ref[p * nc:(p + 1) * nc, :] = y_ref[p].T

    y = yt_ref[...]
    a = jnp.dot(y, wa_ref[...], preferred_element_type=F32)
    b = jnp.dot(y, wb_ref[...], preferred_element_type=F32)
    branch = a * jax.nn.sigmoid(b)
    for p in range(n_ph):
        rows = slice(p * nc, (p + 1) * nc)
        o_ref[p] = (at_ref[p].astype(F32) + gs_ref[p].astype(F32) * branch[rows]).astype(o_ref.dtype)


def _glu_merge(y_t, w_glu, attn_phase, gs_phase, *, phases=2, tn=512):
    _, width, nc = y_t.shape
    d_model = attn_phase.shape[-1]
    nj = d_model // tn
    blk = pl.BlockSpec((phases, nc, tn), lambda t, j: (t, 0, j))
    return pl.pallas_call(
        _glu_kernel,
        grid=(CHUNK // phases, nj),
        in_specs=[
            pl.BlockSpec((phases, width, nc), lambda t, j: (t, 0, 0)),
            pl.BlockSpec((width, tn), lambda t, j: (0, j)),
            pl.BlockSpec((width, tn), lambda t, j: (0, nj + j)),
            blk, blk,
        ],
        out_specs=blk,
        out_shape=jax.ShapeDtypeStruct(attn_phase.shape, BF16),
        scratch_shapes=[pltpu.VMEM((phases * nc, width), BF16)],
        compiler_params=_cparams("arbitrary", "arbitrary"),
        name="glu",
    )(y_t, w_glu, w_glu, attn_phase, gs_phase)


def _outp_kernel(a_ref, w_ref, x_ref, g_ref, o_ref, nat_ref):
    grp = CHUNK * CHUNK

    @pl.when(pl.program_id(1) == 0)
    def _():
        for q in range(nat_ref.shape[0] // grp):
            a = jnp.swapaxes(a_ref[:, q * CHUNK:(q + 1) * CHUNK, :], 0, 1)
            nat_ref[q * grp:(q + 1) * grp, :] = a.reshape(grp, a_ref.shape[-1])

    acc = jnp.dot(nat_ref[...], w_ref[...], preferred_element_type=F32)
    o_ref[...] = x_ref[...] + g_ref[...] * acc


def _out_proj(a_phase, w, x, mod, gate_idx, *, tm=1024, tn=512):
    n_ph, nc, k = a_phase.shape
    s, n = x.shape
    tm = min(tm, s)
    return pl.pallas_call(
        _outp_kernel,
        grid=(s // tm, n // tn),
        in_specs=[
            pl.BlockSpec((n_ph, tm // n_ph, k), lambda i, j: (0, i, 0)),
            pl.BlockSpec((k, tn), lambda i, j: (0, j)),
            pl.BlockSpec((tm, tn), lambda i, j: (i, j)),
            pl.BlockSpec((1, tn), lambda i, j: (0, gate_idx * (n // tn) + j)),
        ],
        out_specs=pl.BlockSpec((tm, tn), lambda i, j: (i, j)),
        out_shape=jax.ShapeDtypeStruct((s, n), F32),
        scratch_shapes=[pltpu.VMEM((tm, k), BF16)],
        compiler_params=_cparams("arbitrary", "arbitrary"),
        name="outp",
    )(a_phase, w, x, mod)


def _ff1_kernel(h_ref, w_ref, o_ref):
    acc = jnp.dot(h_ref[...], w_ref[...].astype(BF16), preferred_element_type=F32)
    r = jnp.maximum(acc, 0.0)
    o_ref[...] = (r * r).astype(o_ref.dtype)


def _ff1(h, w, *, tm=1024, tn=512):
    s, k = h.shape
    n = w.shape[1]
    tm = min(tm, s)
    return pl.pallas_call(
        _ff1_kernel,
        grid=(s // tm, n // tn),
        in_specs=[pl.BlockSpec((tm, k), lambda i, j: (i, 0)), pl.BlockSpec((k, tn), lambda i, j: (0, j))],
        out_specs=pl.BlockSpec((tm, tn), lambda i, j: (i, j)),
        out_shape=jax.ShapeDtypeStruct((s, n), BF16),
        compiler_params=_cparams("arbitrary", "arbitrary"),
        name="ff1",
    )(h, w)


def _ff2_kernel(a_ref, w_ref, x_ref, g_ref, o_ref):
    kk = pl.program_id(2)
    last = pl.num_programs(2) - 1

    def product():
        return jnp.dot(a_ref[...], w_ref[...].astype(BF16), preferred_element_type=F32)

    @pl.when(kk == 0)
    def _():
        o_ref[...] = product()

    @pl.when((kk > 0) & (kk < last))
    def _():
        o_ref[...] += product()

    @pl.when(kk == last)
    def _():
        o_ref[...] = x_ref[...] + g_ref[...] * (o_ref[...] + product())


def _ff2(a, w, x, mod, gate_idx, *, tm=2048, tn=1024, tk=1024):
    s, k = a.shape
    n = w.shape[1]
    tm = min(tm, s)
    assert k // tk >= 2
    return pl.pallas_call(
        _ff2_kernel,
        grid=(s // tm, n // tn, k // tk),
        in_specs=[
            pl.BlockSpec((tm, tk), lambda i, j, kk: (i, kk)),
            pl.BlockSpec((tk, tn), lambda i, j, kk: (kk, j)),
            pl.BlockSpec((tm, tn), lambda i, j, kk: (i, j)),
            pl.BlockSpec((1, tn), lambda i, j, kk: (0, gate_idx * (n // tn) + j)),
        ],
        out_specs=pl.BlockSpec((tm, tn), lambda i, j, kk: (i, j)),
        out_shape=jax.ShapeDtypeStruct((s, n), F32),
        compiler_params=_cparams("arbitrary", "arbitrary", "arbitrary"),
        name="ff2",
    )(a, w, x, mod)


def _layer(x, c_col, pos_col, w_ada, b_ada, norm1_g, norm2_g, w_in, q_norm_g, k_norm_g, attn_sinks,
           lam_re, lam_im, log_dt, b_re, b_im, c_re, c_im, ssm_d, w_glu, w_out, w_ff1, w_ff2):
    s, d = x.shape
    kv_width = N_KV_HEADS * HEAD_DIM
    ssm_width = ssm_d.shape[0]
    n_groups = ssm_width // SSM_GROUP
    nc = s // CHUNK
    n_levels = max(1, (nc - 1).bit_length())

    b_row = b_ada.reshape(1, -1)
    mod_a, c_bcast = _modulation(c_col, w_ada, b_row, 2 * d)
    h, h_phase = _norm_mod(x, norm1_g.reshape(1, d), mod_a, 0, with_phase_major=True)
    rope_c, rope_sa, rope_sb = _rope_tables(pos_col)
    pz, mod_b = _proj(h, w_in, q_norm_g.reshape(1, -1), k_norm_g.reshape(1, -1), rope_c, rope_sa, rope_sb,
                      c_bcast, w_ada, b_row, 2 * d, d_model=d, kv_width=kv_width, ssm_width=ssm_width)
    u_col0 = d + 2 * kv_width
    u_t, gs_phase, w_glu_bf, w_out_bf = _proj_u(h_phase, w_in, w_glu, w_out, u_col0=u_col0,
                                                gs_col0=u_col0 + ssm_width + d, ssm_width=ssm_width, d_model=d)

    s5_ops = _s5_prep(log_dt, lam_re, lam_im, b_re, b_im, c_re, c_im, n_levels=n_levels)
    d_col = jnp.tile(ssm_d.reshape(n_groups, 1, SSM_GROUP), (1, CHUNK, 1)).reshape(n_groups, CHUNK * SSM_GROUP, 1)
    y_t = _s5_scan(u_t, s5_ops, d_col, n_levels=n_levels)

    attn_phase = _attention(attn_sinks, pz, d_model=d, kv_width=kv_width)
    merged_phase = _glu_merge(y_t, w_glu_bf, attn_phase, gs_phase)
    x1 = _out_proj(merged_phase, w_out_bf, x, mod_b, 0)

    (h2,) = _norm_mod(x1, norm2_g.reshape(1, d), mod_b, 1)
    act = _ff1(h2, w_ff1)
    return _ff2(act, w_ff2, x1, mod_b, 3)


def kernel(x, c, positions, w_ada, b_ada, norm1_g, norm2_g, w_in, q_norm_g, k_norm_g, attn_sinks, ssm_lam_re,
           ssm_lam_im, ssm_log_dt, ssm_b_re, ssm_b_im, ssm_c_re, ssm_c_im, ssm_d, w_glu, w_out, w_ff1, w_ff2):
    bsz, s, d = x.shape
    assert bsz == 1 and d == N_Q_HEADS * HEAD_DIM and s % (CHUNK * LANES) == 0
    xs = x[0]
    c_col = c.reshape(d, 1)
    pos_col = positions.reshape(s, 1)
    for l in range(w_ada.shape[0]):
        xs = _layer(xs, c_col, pos_col, w_ada[l], b_ada[l], norm1_g[l], norm2_g[l], w_in[l], q_norm_g[l],
                    k_norm_g[l], attn_sinks[l], ssm_lam_re[l], ssm_lam_im[l], ssm_log_dt[l], ssm_b_re[l],
                    ssm_b_im[l], ssm_c_re[l], ssm_c_im[l], ssm_d[l], w_glu[l], w_out[l], w_ff1[l], w_ff2[l])
    return xs[None]
```

```python
import functools
import math

import jax
import jax.numpy as jnp
from jax import lax
from jax.experimental import pallas as pl
from jax.experimental.pallas import tpu as pltpu

F32 = jnp.float32
BF16 = jnp.bfloat16

LANES = 128
SUBLANES = 8
MXU_DIM = 256
VMEM_LIMIT_BYTES = 56 * 1024 * 1024

N_Q_HEADS = 32
N_KV_HEADS = 8
HEAD_DIM = 128
REP = N_Q_HEADS // N_KV_HEADS
WINDOW = 128
ROPE_DIM = HEAD_DIM // 4
ROPE_HALF = ROPE_DIM // 2
ROPE_THETA = 500000.0
SSM_GROUP = 16
SSM_STATE = 64
N_MOD = 6
EPS = 1e-6
MASK_VALUE = -1e30
CHUNK = MXU_DIM // SSM_GROUP


def _phase_perm():
    n = CHUNK * CHUNK
    r = lax.broadcasted_iota(jnp.int32, (n, n), 0)
    c = lax.broadcasted_iota(jnp.int32, (n, n), 1)
    return jnp.where((r % CHUNK) * CHUNK + r // CHUNK == c, 1.0, 0.0).astype(BF16)


def _to_phase_major(v, perm):
    return jnp.dot(perm, v, preferred_element_type=F32).astype(v.dtype).reshape(CHUNK, CHUNK, v.shape[-1])


def _lagged_tiles(n_tiles, n_inner):
    def cur(step):
        t = jnp.minimum(step, n_tiles - 1)
        return t // n_inner, t % n_inner

    def prev(step):
        t = jnp.maximum(step - 1, 0)
        return t // n_inner, t % n_inner

    return cur, prev


def _overlapped_step(step, run_matmul, epilogues, accs):
    cases = [(step == 0, None)] + [((step > 0) & cond, fn) for cond, fn in epilogues]
    for parity in (0, 1):
        for cond, fn in cases:
            @pl.when(cond & (step % 2 == parity))
            def _(dst=accs[parity], src=accs[1 - parity], fn=fn):
                if fn is not None:
                    fn(src)
                run_matmul(dst)


def _side_slabs(n_tiles, rows):
    n_side = 1 << (n_tiles.bit_length() - 1)
    assert rows % (n_side * 2 * SUBLANES) == 0
    return n_side, rows // n_side


def _cparams(*sem):
    return pltpu.CompilerParams(dimension_semantics=sem, vmem_limit_bytes=VMEM_LIMIT_BYTES)


def _mod_kernel(c_ref, w_ref, b_ref, o_ref, cb_ref, *, rk):
    d, tn = w_ref.shape

    def body(r, acc):
        rows = pl.ds(pl.multiple_of(r * rk, rk), rk)
        c = c_ref[rows, :]
        ca = c * jax.nn.sigmoid(c)
        cb_ref[rows, :] = jnp.broadcast_to(ca, (rk, LANES))
        prod = w_ref[rows, :] * ca
        return acc + jnp.sum(prod.reshape(rk // SUBLANES, SUBLANES, tn), axis=0)

    acc = lax.fori_loop(0, d // rk, body, jnp.zeros((SUBLANES, tn), F32))
    o_ref[...] = jnp.sum(acc, axis=0, keepdims=True) + b_ref[...]


def _modulation(c_col, w_ada, b_ada_row, n_cols, *, tn=512, rk=512):
    d = w_ada.shape[0]
    return pl.pallas_call(
        functools.partial(_mod_kernel, rk=rk),
        grid=(n_cols // tn,),
        in_specs=[
            pl.BlockSpec((d, 1), lambda j: (0, 0)),
            pl.BlockSpec((d, tn), lambda j: (0, j)),
            pl.BlockSpec((1, tn), lambda j: (0, j)),
        ],
        out_specs=[pl.BlockSpec((1, tn), lambda j: (0, j)), pl.BlockSpec((d, LANES), lambda j: (0, 0))],
        out_shape=[jax.ShapeDtypeStruct((1, n_cols), F32), jax.ShapeDtypeStruct((d, LANES), F32)],
        compiler_params=_cparams("arbitrary"),
        name="mod",
    )(c_col, w_ada, b_ada_row)


def _mod_side_task(block, bias_col0, cb_ref, wm_ref, bm_ref, om_ref):
    d, wc = wm_ref.shape
    cb = cb_ref[...]
    for q in range(wc // LANES):
        part = jnp.sum((wm_ref[:, q * LANES:(q + 1) * LANES] * cb).reshape(d // SUBLANES, SUBLANES, LANES), axis=0)
        col = block * wc + q * LANES
        bias = bm_ref[:, pl.ds(pl.multiple_of(bias_col0 + col, LANES), LANES)]
        om_ref[:, pl.ds(pl.multiple_of(col, LANES), LANES)] = jnp.sum(part, axis=0, keepdims=True) + bias


def _norm_kernel(x_ref, g_ref, sh_ref, sc_ref, o_ref, *phase_ref):
    def normed(x):
        ms = jnp.mean(x * x, axis=-1, keepdims=True)
        y = x * lax.rsqrt(ms + EPS) * g_ref[...]
        return (y * (1.0 + sc_ref[...]) + sh_ref[...]).astype(o_ref.dtype)

    hb = normed(x_ref[...])
    o_ref[...] = hb
    for oph_ref in phase_ref:
        perm = _phase_perm()
        grp = CHUNK * CHUNK
        for q in range(hb.shape[0] // grp):
            oph_ref[:, q * CHUNK:(q + 1) * CHUNK, :] = _to_phase_major(hb[q * grp:(q + 1) * grp], perm)


def _norm_mod(x, g_row, mod, shift_idx, *, with_phase_major=False, tm=512):
    s, d = x.shape
    out_specs = [pl.BlockSpec((tm, d), lambda i: (i, 0))]
    out_shape = [jax.ShapeDtypeStruct((s, d), BF16)]
    if with_phase_major:
        out_specs.append(pl.BlockSpec((CHUNK, tm // CHUNK, d), lambda i: (0, i, 0)))
        out_shape.append(jax.ShapeDtypeStruct((CHUNK, s // CHUNK, d), BF16))
    return pl.pallas_call(
        _norm_kernel,
        grid=(s // tm,),
        in_specs=[
            pl.BlockSpec((tm, d), lambda i: (i, 0)),
            pl.BlockSpec((1, d), lambda i: (0, 0)),
            pl.BlockSpec((1, d), lambda i: (0, shift_idx)),
            pl.BlockSpec((1, d), lambda i: (0, shift_idx + 1)),
        ],
        out_specs=out_specs,
        out_shape=out_shape,
        compiler_params=_cparams("arbitrary"),
        name="norm",
    )(x, g_row, mod, mod)


def _rope_kernel(pos_ref, c_ref, sa_ref, sb_ref):
    pos = pos_ref[...].astype(F32)
    lane = lax.broadcasted_iota(jnp.int32, (1, LANES), 1)
    idx = (lane & (ROPE_HALF - 1)).astype(F32)
    inv_freq = jnp.power(jnp.float32(ROPE_THETA), -2.0 * idx / ROPE_DIM)
    ang = pos * inv_freq
    cos = jnp.cos(ang)
    sin = jnp.sin(ang)
    c_ref[...] = jnp.where(lane < ROPE_DIM, cos, 1.0)
    sa_ref[...] = jnp.where(lane < ROPE_HALF, -sin, 0.0)
    sb_ref[...] = jnp.where((lane >= ROPE_HALF) & (lane < ROPE_DIM), sin, 0.0)


def _rope_tables(pos_col, *, tm=1024):
    s = pos_col.shape[0]
    tm = min(tm, s)
    spec = pl.BlockSpec((tm, LANES), lambda i: (i, 0))
    shp = jax.ShapeDtypeStruct((s, LANES), F32)
    return pl.pallas_call(
        _rope_kernel,
        grid=(s // tm,),
        in_specs=[pl.BlockSpec((tm, 1), lambda i: (i, 0))],
        out_specs=[spec, spec, spec],
        out_shape=[shp, shp, shp],
        compiler_params=_cparams("arbitrary"),
        name="rope",
    )(pos_col)


def _proj_kernel(h_ref, w_ref, qg_ref, kg_ref, rc_ref, rsa_ref, rsb_ref, cb_ref, wm_ref, bm_ref, o_ref, om_ref,
                 acc0_ref, acc1_ref, *, n_j, n_q, n_qk, n_qkv, n_side, mod_col0):
    step = pl.program_id(0)
    jp = (step - 1) % n_j
    tn = o_ref.shape[1]

    def qk_epilogue(src):
        g = jnp.where(jp < n_q, qg_ref[...], kg_ref[...])
        c, sa, sb = rc_ref[...], rsa_ref[...], rsb_ref[...]
        for hh in range(tn // HEAD_DIM):
            cols = slice(hh * HEAD_DIM, (hh + 1) * HEAD_DIM)
            xh = src[:, cols]
            ms = jnp.mean(xh * xh, axis=-1, keepdims=True)
            y = xh * lax.rsqrt(ms + EPS) * g
            y = y * c + pltpu.roll(y, HEAD_DIM - ROPE_HALF, 1) * sa + pltpu.roll(y, ROPE_HALF, 1) * sb
            o_ref[:, cols] = y.astype(o_ref.dtype)

    def v_epilogue(src):
        o_ref[...] = src[...].astype(o_ref.dtype)

    def gate_epilogue(src):
        o_ref[...] = jax.nn.sigmoid(src[...]).astype(o_ref.dtype)

    def run_matmul(dst):
        _mod_side_task(jnp.minimum(step, n_side - 1), mod_col0, cb_ref, wm_ref, bm_ref, om_ref)
        dst[...] = jnp.dot(h_ref[...], w_ref[...].astype(BF16), preferred_element_type=F32)

    _overlapped_step(step, run_matmul,
                     [(jp < n_qk, qk_epilogue), ((jp >= n_qk) & (jp < n_qkv), v_epilogue), (jp >= n_qkv, gate_epilogue)],
                     (acc0_ref, acc1_ref))


def _proj(h, w_in, qg_row, kg_row, rope_c, rope_sa, rope_sb, c_bcast, w_ada, b_ada_row, mod_col0, *,
          d_model, kv_width, ssm_width, tm=1024, tn=512):
    s, d = h.shape
    tm = min(tm, s)
    n_q = d_model // tn
    n_qk = n_q + kv_width // tn
    n_qkv = n_qk + kv_width // tn
    n_u = ssm_width // tn
    n_j = n_qkv + d_model // tn
    n_tiles = (s // tm) * n_j
    cur, prev = _lagged_tiles(n_tiles, n_j)

    def w_map(t):
        j = cur(t)[1]
        return 0, jnp.where(j < n_qkv, j, j + n_u)

    n_mod = w_ada.shape[1] - mod_col0
    wc = LANES * pl.cdiv(n_mod // LANES, n_tiles)
    n_side = n_mod // wc
    assert n_mod % wc == 0 and mod_col0 % wc == 0

    def side(t):
        return jnp.minimum(t, n_side - 1)

    tab = pl.BlockSpec((tm, LANES), lambda t: (prev(t)[0], 0))
    row = pl.BlockSpec((1, HEAD_DIM), lambda t: (0, 0))
    return pl.pallas_call(
        functools.partial(_proj_kernel, n_j=n_j, n_q=n_q, n_qk=n_qk, n_qkv=n_qkv, n_side=n_side,
                          mod_col0=mod_col0),
        grid=(n_tiles + 1,),
        in_specs=[
            pl.BlockSpec((tm, d), lambda t: (cur(t)[0], 0)),
            pl.BlockSpec((d, tn), w_map),
            row, row, tab, tab, tab,
            pl.BlockSpec((d, LANES), lambda t: (0, 0)),
            pl.BlockSpec((d, wc), lambda t: (0, mod_col0 // wc + side(t))),
            pl.BlockSpec((1, w_ada.shape[1]), lambda t: (0, 0)),
        ],
        out_specs=[pl.BlockSpec((tm, tn), lambda t: prev(t)), pl.BlockSpec((1, n_mod), lambda t: (0, 0))],
        out_shape=[jax.ShapeDtypeStruct((s, n_j * tn), BF16), jax.ShapeDtypeStruct((1, n_mod), F32)],
        scratch_shapes=[pltpu.VMEM((tm, tn), F32), pltpu.VMEM((tm, tn), F32)],
        compiler_params=_cparams("arbitrary"),
        name="proj",
    )(h, w_in, qg_row, kg_row, rope_c, rope_sa, rope_sb, c_bcast, w_ada, b_ada_row)


def _proju_kernel(h_ref, w_ref, wglu_ref, wout_ref, ut_ref, gs_ref, wglu_bf_ref, wout_bf_ref, acc0_ref, acc1_ref, *,
                  n_j, n_u):
    step = pl.program_id(0)
    jp = (step - 1) % n_j
    n_ph, nc, d = h_ref.shape

    def u_epilogue(src):
        for p in range(n_ph):
            ut_ref[p] = src[p * nc:(p + 1) * nc, :].T

    def gate_epilogue(src):
        for p in range(n_ph):
            gs_ref[p] = jax.nn.sigmoid(src[p * nc:(p + 1) * nc, :]).astype(gs_ref.dtype)

    def run_matmul(dst):
        wglu_bf_ref[...] = wglu_ref[...].astype(wglu_bf_ref.dtype)
        wout_bf_ref[...] = wout_ref[...].astype(wout_bf_ref.dtype)
        dst[...] = jnp.dot(h_ref[...].reshape(n_ph * nc, d), w_ref[...].astype(BF16), preferred_element_type=F32)

    _overlapped_step(step, run_matmul, [(jp < n_u, u_epilogue), (jp >= n_u, gate_epilogue)], (acc0_ref, acc1_ref))


def _proj_u(h_phase, w_in, w_glu, w_out, *, u_col0, gs_col0, ssm_width, d_model, phases=2, tn=512):
    _, nc, d = h_phase.shape
    n_u = ssm_width // tn
    n_j = n_u + d_model // tn
    n_tiles = (CHUNK // phases) * n_j
    cur, prev = _lagged_tiles(n_tiles, n_j)
    n_cast, glu_rows = _side_slabs(n_tiles, w_glu.shape[0])
    _, out_rows = _side_slabs(n_tiles, w_out.shape[0])

    def cast_map(s):
        return jnp.minimum(s, n_cast - 1), 0

    def w_map(s):
        j = cur(s)[1]
        return 0, jnp.where(j < n_u, u_col0 // tn + j, gs_col0 // tn + j - n_u)

    def ut_map(s):
        t, j = prev(s)
        return t, jnp.minimum(j, n_u - 1), 0

    def gs_map(s):
        t, j = prev(s)
        return t, 0, jnp.maximum(j - n_u, 0)

    return pl.pallas_call(
        functools.partial(_proju_kernel, n_j=n_j, n_u=n_u),
        grid=(n_tiles + 1,),
        in_specs=[
            pl.BlockSpec((phases, nc, d), lambda s: (cur(s)[0], 0, 0)),
            pl.BlockSpec((d, tn), w_map),
            pl.BlockSpec((glu_rows, w_glu.shape[1]), cast_map),
            pl.BlockSpec((out_rows, w_out.shape[1]), cast_map),
        ],
        out_specs=[
            pl.BlockSpec((phases, tn, nc), ut_map),
            pl.BlockSpec((phases, nc, tn), gs_map),
            pl.BlockSpec((glu_rows, w_glu.shape[1]), cast_map),
            pl.BlockSpec((out_rows, w_out.shape[1]), cast_map),
        ],
        out_shape=[jax.ShapeDtypeStruct((CHUNK, ssm_width, nc), F32),
                   jax.ShapeDtypeStruct((CHUNK, nc, d_model), BF16),
                   jax.ShapeDtypeStruct(w_glu.shape, BF16),
                   jax.ShapeDtypeStruct(w_out.shape, BF16)],
        scratch_shapes=[pltpu.VMEM((phases * nc, tn), F32), pltpu.VMEM((phases * nc, tn), F32)],
        compiler_params=_cparams("arbitrary"),
        name="proj_u",
    )(h_phase, w_in, w_glu, w_out)


def _cmul(ar, ai, br, bi):
    return ar * br - ai * bi, ar * bi + ai * br


def _s5prep_kernel(ldt_ref, lr_ref, li_ref, bre_ref, bim_ref, cre_ref, cim_ref,
                   m_ref, wre_ref, wim_ref, vre_ref, vim_ref, apre_ref, apim_ref, *, n_levels):
    for gi in range(m_ref.shape[0]):
        _s5prep_group(gi, ldt_ref, lr_ref, li_ref, bre_ref, bim_ref, cre_ref, cim_ref,
                      m_ref, wre_ref, wim_ref, vre_ref, vim_ref, apre_ref, apim_ref, n_levels)


def _s5prep_group(gi, ldt_ref, lr_ref, li_ref, bre_ref, bim_ref, cre_ref, cim_ref,
                  m_ref, wre_ref, wim_ref, vre_ref, vim_ref, apre_ref, apim_ref, n_levels):
    p_ = SSM_STATE
    kdim = CHUNK * SSM_GROUP
    first = lax.broadcasted_iota(jnp.int32, (1, 2 * p_), 1) < p_
    own = first if gi % 2 == 0 else jnp.logical_not(first)
    dt = jnp.exp(ldt_ref[gi])
    lr, li = lr_ref[gi], li_ref[gi]
    mag = jnp.exp(lr * dt)
    are, aim = mag * jnp.cos(li * dt), mag * jnp.sin(li * dt)
    den = lr * lr + li * li
    nr = are - 1.0
    coef_re = (nr * lr + aim * li) / den
    coef_im = (aim * lr - nr * li) / den
    bt_re, bt_im = bre_ref[gi], bim_ref[gi]
    bbar_re = coef_re * bt_re - coef_im * bt_im
    bbar_im = coef_re * bt_im + coef_im * bt_re

    pow_re = [jnp.ones_like(are)]
    pow_im = [jnp.zeros_like(are)]
    for _ in range(CHUNK):
        nre, nim = _cmul(pow_re[-1], pow_im[-1], are, aim)
        pow_re.append(nre)
        pow_im.append(nim)

    blocks = []
    for m in range(CHUNK):
        g_re, g_im = _cmul(pow_re[CHUNK - 1 - m], pow_im[CHUNK - 1 - m], bbar_re, bbar_im)
        rows = slice(m * SSM_GROUP, (m + 1) * SSM_GROUP)
        wre_ref[gi, rows, :] = jnp.where(own, g_re, 0.0).astype(wre_ref.dtype)
        wim_ref[gi, rows, :] = jnp.where(own, g_im, 0.0).astype(wim_ref.dtype)
        blocks.append(jnp.where(first, g_re, g_im))
    g_t = jnp.concatenate(blocks, axis=0)

    cre2, cim2 = cre_ref[gi], cim_ref[gi]
    c_mix = jnp.where(first, cre2, -cim2)
    k_rev = lax.dot_general(c_mix, g_t, (((1,), (1,)), ((), ())), precision=lax.Precision.HIGHEST,
                            preferred_element_type=F32)
    k_ext = jnp.concatenate([k_rev, jnp.zeros_like(k_rev)], axis=1)
    for t in range(CHUNK):
        off = (CHUNK - 1 - t) * SSM_GROUP
        m_ref[gi, t * SSM_GROUP:(t + 1) * SSM_GROUP, :] = k_ext[:, off:off + kdim].astype(m_ref.dtype)

    for t in range(CHUNK):
        pr, pi = pow_re[t + 1], pow_im[t + 1]
        rows = slice(t * SSM_GROUP, (t + 1) * SSM_GROUP)
        vre_ref[gi, rows, :] = jnp.where(own, cre2 * pr - cim2 * pi, 0.0).astype(vre_ref.dtype)
        vim_ref[gi, rows, :] = jnp.where(own, -(cre2 * pi + cim2 * pr), 0.0).astype(vim_ref.dtype)

    sub = lax.broadcasted_iota(jnp.int32, (apre_ref.shape[1], 1), 0)
    qr, qi = pow_re[CHUNK], pow_im[CHUNK]
    lev_re = jnp.zeros(apre_ref.shape[1:], F32)
    lev_im = jnp.zeros(apre_ref.shape[1:], F32)
    for k in range(n_levels):
        lev_re = jnp.where((sub == k) & own, qr, lev_re)
        lev_im = jnp.where((sub == k) & own, qi, lev_im)
        qr, qi = _cmul(qr, qi, qr, qi)
    apre_ref[gi] = lev_re
    apim_ref[gi] = lev_im


def _s5_prep(log_dt, lam_re, lam_im, b_re, b_im, c_re, c_im, *, n_levels, gb=16):
    g, p_ = lam_re.shape
    assert gb % 2 == 0 and n_levels <= 2 * SUBLANES

    def dup(a):
        return jnp.concatenate([a, a], axis=-1)

    ldt = log_dt.reshape(g, 1, 1)
    lr, li = dup(lam_re).reshape(g, 1, 2 * p_), dup(lam_im).reshape(g, 1, 2 * p_)
    bt_re, bt_im = dup(jnp.swapaxes(b_re, 1, 2)), dup(jnp.swapaxes(b_im, 1, 2))
    kdim = CHUNK * SSM_GROUP

    def spec(*shape):
        return pl.BlockSpec((gb,) + shape, lambda i: (i, 0, 0))

    op = jax.ShapeDtypeStruct((g, kdim, 2 * p_), BF16)
    lev = jax.ShapeDtypeStruct((g, 2 * SUBLANES, 2 * p_), F32)
    return pl.pallas_call(
        functools.partial(_s5prep_kernel, n_levels=n_levels),
        grid=(g // gb,),
        in_specs=[spec(1, 1), spec(1, 2 * p_), spec(1, 2 * p_), spec(SSM_GROUP, 2 * p_), spec(SSM_GROUP, 2 * p_),
                  spec(SSM_GROUP, 2 * p_), spec(SSM_GROUP, 2 * p_)],
        out_specs=[spec(kdim, kdim)] + [spec(kdim, 2 * p_)] * 4 + [spec(2 * SUBLANES, 2 * p_)] * 2,
        out_shape=[jax.ShapeDtypeStruct((g, kdim, kdim), BF16), op, op, op, op, lev, lev],
        compiler_params=_cparams("arbitrary"),
        name="s5prep",
    )(ldt, lr, li, bt_re, bt_im, dup(c_re), dup(c_im))


def _s5_kernel(x_ref, m_ref, wre_ref, wim_ref, vre_ref, vim_ref, apre_ref, apim_ref, d_ref, o_ref, *, n_levels):
    for pair in range(m_ref.shape[0] // 2):
        _s5_pair(pair, x_ref, m_ref, wre_ref, wim_ref, vre_ref, vim_ref, apre_ref, apim_ref, d_ref, o_ref, n_levels)


def _s5_pair(pair, x_ref, m_ref, wre_ref, wim_ref, vre_ref, vim_ref, apre_ref, apim_ref, d_ref, o_ref, n_levels):
    kdim = CHUNK * SSM_GROUP
    nc = x_ref.shape[-1]
    groups = (2 * pair, 2 * pair + 1)
    chans = [slice(g * SSM_GROUP, (g + 1) * SSM_GROUP) for g in groups]
    xs = [x_ref[:, c, :].reshape(kdim, nc) for c in chans]
    xbs = [x.astype(BF16) for x in xs]
    xcat = jnp.concatenate(xbs, axis=0)
    tn_ = (((0,), (0,)), ((), ()))
    nt = (((1,), (1,)), ((), ()))

    def pair_rows(ref):
        return jnp.concatenate([ref[g] for g in groups], axis=0)

    s_re = lax.dot_general(xcat, pair_rows(wre_ref), tn_, preferred_element_type=F32)
    s_im = lax.dot_general(xcat, pair_rows(wim_ref), tn_, preferred_element_type=F32)
    row = lax.broadcasted_iota(jnp.int32, (nc, 1), 0)

    def shift_rows(v, sh):
        if sh % SUBLANES == 0:
            return jnp.concatenate([jnp.zeros((sh, v.shape[1]), v.dtype), v[:nc - sh]], axis=0)
        return jnp.where(row >= sh, pltpu.roll(v, sh, 0), 0.0)

    lev_re = apre_ref[groups[0]] + apre_ref[groups[1]]
    lev_im = apim_ref[groups[0]] + apim_ref[groups[1]]
    for k in range(n_levels):
        sh = 1 << k
        pr, pi = lev_re[k:k + 1], lev_im[k:k + 1]
        t_re, t_im = shift_rows(s_re, sh), shift_rows(s_im, sh)
        s_re, s_im = s_re + (pr * t_re - pi * t_im), s_im + (pr * t_im + pi * t_re)
    st_re = shift_rows(s_re, 1).astype(BF16)
    st_im = shift_rows(s_im, 1).astype(BF16)

    for g, c, x, xb in zip(groups, chans, xs, xbs):
        y = jnp.dot(m_ref[g], xb, preferred_element_type=F32)
        y = y + lax.dot_general(vre_ref[g], st_re, nt, preferred_element_type=F32)
        y = y + lax.dot_general(vim_ref[g], st_im, nt, preferred_element_type=F32)
        y = y + d_ref[g] * x
        o_ref[:, c, :] = jax.nn.gelu(y).astype(o_ref.dtype).reshape(CHUNK, SSM_GROUP, nc)


def _s5_scan(u_t, ops, d_col, *, n_levels, gb=8):
    _, width, nc = u_t.shape
    g = width // SSM_GROUP
    kdim = CHUNK * SSM_GROUP
    p2 = 2 * SSM_STATE

    def spec(*shape):
        return pl.BlockSpec((gb,) + shape, lambda i: (i, 0, 0))

    io = pl.BlockSpec((CHUNK, gb * SSM_GROUP, nc), lambda i: (0, i, 0))
    return pl.pallas_call(
        functools.partial(_s5_kernel, n_levels=n_levels),
        grid=(g // gb,),
        in_specs=[io, spec(kdim, kdim)] + [spec(kdim, p2)] * 4 + [spec(2 * SUBLANES, p2)] * 2 + [spec(kdim, 1)],
        out_specs=io,
        out_shape=jax.ShapeDtypeStruct((CHUNK, width, nc), BF16),
        compiler_params=_cparams("arbitrary"),
        name="s5",
    )(u_t, *ops, d_col)


def _attn_kernel(sink_ref, q_ref, kc_ref, kp_ref, vc_ref, vp_ref, ga_ref, o_ref, nat_ref, *, n_groups):
    i = pl.program_id(0)
    gh = pl.program_id(1)
    tq = q_ref.shape[0]
    nb = tq // WINDOW
    scale = HEAD_DIM ** -0.5
    row = lax.broadcasted_iota(jnp.int32, (WINDOW, WINDOW), 0)
    col = lax.broadcasted_iota(jnp.int32, (WINDOW, WINDOW), 1)
    tri1 = col <= row
    tri = jnp.concatenate([tri1] * REP, axis=0)
    nt = (((1,), (1,)), ((), ()))
    for b in range(nb):
        rows = slice(b * WINDOW, (b + 1) * WINDOW)
        has_prev = (i * nb + b) > 0
        valid = tri | has_prev
        for g in range(n_groups):
            kcols = slice(g * HEAD_DIM, (g + 1) * HEAD_DIM)
            if b == 0:
                k_prev, v_prev = kp_ref[:, kcols], vp_ref[:, kcols]
            else:
                prows = slice((b - 1) * WINDOW, b * WINDOW)
                k_prev, v_prev = kc_ref[prows, kcols], vc_ref[prows, kcols]
            k_band = jnp.concatenate([k_prev, kc_ref[rows, kcols]], axis=0)
            v_band = jnp.concatenate([v_prev, vc_ref[rows, kcols]], axis=0)
            qcols = [slice((g * REP + r) * HEAD_DIM, (g * REP + r + 1) * HEAD_DIM) for r in range(REP)]
            qg = jnp.concatenate([q_ref[rows, c] for c in qcols], axis=0)
            s2 = lax.dot_general(qg, k_band, nt, preferred_element_type=F32)
            s = jnp.where(tri, s2[:, WINDOW:], s2[:, :WINDOW]) * scale
            s = jnp.where(valid, s, MASK_VALUE)
            probs = []
            for r in range(REP):
                sink = sink_ref[(gh * n_groups + g) * REP + r]
                sr = s[r * WINDOW:(r + 1) * WINDOW]
                m = jnp.maximum(jnp.max(sr, axis=-1, keepdims=True), sink)
                p = jnp.exp(sr - m)
                denom = jnp.sum(p, axis=-1, keepdims=True) + jnp.exp(sink - m)
                probs.append(p * (1.0 / denom))
            pr = jnp.concatenate(probs, axis=0)
            p2 = jnp.concatenate([jnp.where(tri, 0.0, pr), jnp.where(tri, pr, 0.0)], axis=1).astype(BF16)
            o = jnp.dot(p2, v_band, preferred_element_type=F32)
            for r in range(REP):
                gate = ga_ref[rows, qcols[r]].astype(F32)
                nat_ref[rows, qcols[r]] = gate * o[r * WINDOW:(r + 1) * WINDOW]
    perm = _phase_perm()
    grp = CHUNK * CHUNK
    for q in range(tq // grp):
        nat = nat_ref[q * grp:(q + 1) * grp, :].astype(o_ref.dtype)
        o_ref[:, q * CHUNK:(q + 1) * CHUNK, :] = _to_phase_major(nat, perm)


def _attention(sinks, pz, *, d_model, kv_width, tq=512, n_groups=4):
    s = pz.shape[0]
    tq = min(tq, s)
    nb = tq // WINDOW
    wq = n_groups * REP * HEAD_DIM
    wk = n_groups * HEAD_DIM
    k0 = d_model // wk
    v0 = (d_model + kv_width) // wk
    a0 = (d_model + 2 * kv_width) // wq

    def prev_map(col0):
        return lambda i, gh: (jnp.maximum(i * nb - 1, 0), col0 + gh)

    return pl.pallas_call(
        functools.partial(_attn_kernel, n_groups=n_groups),
        grid=(s // tq, d_model // wq),
        in_specs=[
            pl.BlockSpec(memory_space=pltpu.SMEM),
            pl.BlockSpec((tq, wq), lambda i, gh: (i, gh)),
            pl.BlockSpec((tq, wk), lambda i, gh: (i, k0 + gh)),
            pl.BlockSpec((WINDOW, wk), prev_map(k0)),
            pl.BlockSpec((tq, wk), lambda i, gh: (i, v0 + gh)),
            pl.BlockSpec((WINDOW, wk), prev_map(v0)),
            pl.BlockSpec((tq, wq), lambda i, gh: (i, a0 + gh)),
        ],
        out_specs=pl.BlockSpec((CHUNK, tq // CHUNK, wq), lambda i, gh: (0, i, gh)),
        out_shape=jax.ShapeDtypeStruct((CHUNK, s // CHUNK, d_model), BF16),
        scratch_shapes=[pltpu.VMEM((tq, wq), F32)],
        compiler_params=_cparams("arbitrary", "arbitrary"),
        name="attn",
    )(sinks, pz, pz, pz, pz, pz, pz)


def _glu_kernel(y_ref, wa_ref, wb_ref, at_ref, gs_ref, o_ref, yt_ref):
    n_ph, _, nc = y_ref.shape

    @pl.when(pl.program_id(1) == 0)
    def _():
        for p in range(n_ph):
            yt_ref[p * nc:(p + 1) * nc, :] = y_ref[p].T

    y = yt_ref[...]
    a = jnp.dot(y, wa_ref[...], preferred_element_type=F32)
    b = jnp.dot(y, wb_ref[...], preferred_element_type=F32)
    branch = a * jax.nn.sigmoid(b)
    for p in range(n_ph):
        rows = slice(p * nc, (p + 1) * nc)
        o_ref[p] = (at_ref[p].astype(F32) + gs_ref[p].astype(F32) * branch[rows]).astype(o_ref.dtype)


def _glu_merge(y_t, w_glu, attn_phase, gs_phase, *, phases=2, tn=512):
    _, width, nc = y_t.shape
    d_model = attn_phase.shape[-1]
    nj = d_model // tn
    blk = pl.BlockSpec((phases, nc, tn), lambda t, j: (t, 0, j))
    return pl.pallas_call(
        _glu_kernel,
        grid=(CHUNK // phases, nj),
        in_specs=[
            pl.BlockSpec((phases, width, nc), lambda t, j: (t, 0, 0)),
            pl.BlockSpec((width, tn), lambda t, j: (0, j)),
            pl.BlockSpec((width, tn), lambda t, j: (0, nj + j)),
            blk, blk,
        ],
        out_specs=blk,
        out_shape=jax.ShapeDtypeStruct(attn_phase.shape, BF16),
        scratch_shapes=[pltpu.VMEM((phases * nc, width), BF16)],
        compiler_params=_cparams("arbitrary", "arbitrary"),
        name="glu",
    )(y_t, w_glu, w_glu, attn_phase, gs_phase)


def _outp_kernel(a_ref, w_ref, x_ref, g_ref, o_ref, nat_ref):
    grp = CHUNK * CHUNK

    @pl.when(pl.program_id(1) == 0)
    def _():
        for q in range(nat_ref.shape[0] // grp):
            a = jnp.swapaxes(a_ref[:, q * CHUNK:(q + 1) * CHUNK, :], 0, 1)
            nat_ref[q * grp:(q + 1) * grp, :] = a.reshape(grp, a_ref.shape[-1])

    acc = jnp.dot(nat_ref[...], w_ref[...], preferred_element_type=F32)
    o_ref[...] = x_ref[...] + g_ref[...] * acc


def _out_proj(a_phase, w, x, mod, gate_idx, *, tm=1024, tn=512):
    n_ph, nc, k = a_phase.shape
    s, n = x.shape
    tm = min(tm, s)
    return pl.pallas_call(
        _outp_kernel,
        grid=(s // tm, n // tn),
        in_specs=[
            pl.BlockSpec((n_ph, tm // n_ph, k), lambda i, j: (0, i, 0)),
            pl.BlockSpec((k, tn), lambda i, j: (0, j)),
            pl.BlockSpec((tm, tn), lambda i, j: (i, j)),
            pl.BlockSpec((1, tn), lambda i, j: (0, gate_idx * (n // tn) + j)),
        ],
        out_specs=pl.BlockSpec((tm, tn), lambda i, j: (i, j)),
        out_shape=jax.ShapeDtypeStruct((s, n), F32),
        scratch_shapes=[pltpu.VMEM((tm, k), BF16)],
        compiler_params=_cparams("arbitrary", "arbitrary"),
        name="outp",
    )(a_phase, w, x, mod)


def _ff1_kernel(h_ref, w_ref, o_ref):
    acc = jnp.dot(h_ref[...], w_ref[...].astype(BF16), preferred_element_type=F32)
    r = jnp.maximum(acc, 0.0)
    o_ref[...] = (r * r).astype(o_ref.dtype)


def _ff1(h, w, *, tm=1024, tn=512):
    s, k = h.shape
    n = w.shape[1]
    tm = min(tm, s)
    return pl.pallas_call(
        _ff1_kernel,
        grid=(s // tm, n // tn),
        in_specs=[pl.BlockSpec((tm, k), lambda i, j: (i, 0)), pl.BlockSpec((k, tn), lambda i, j: (0, j))],
        out_specs=pl.BlockSpec((tm, tn), lambda i, j: (i, j)),
        out_shape=jax.ShapeDtypeStruct((s, n), BF16),
        compiler_params=_cparams("arbitrary", "arbitrary"),
        name="ff1",
    )(h, w)


def _ff2_kernel(a_ref, w_ref, x_ref, g_ref, o_ref):
    kk = pl.program_id(2)
    last = pl.num_programs(2) - 1

    def product():
        return jnp.dot(a_ref[...], w_ref[...].astype(BF16), preferred_element_type=F32)

    @pl.when(kk == 0)
    def _():
        o_ref[...] = product()

    @pl.when((kk > 0) & (kk < last))
    def _():
        o_ref[...] += product()

    @pl.when(kk == last)
    def _():
        o_ref[...] = x_ref[...] + g_ref[...] * (o_ref[...] + product())


def _ff2(a, w, x, mod, gate_idx, *, tm=2048, tn=1024, tk=1024):
    s, k = a.shape
    n = w.shape[1]
    tm = min(tm, s)
    assert k // tk >= 2
    return pl.pallas_call(
        _ff2_kernel,
        grid=(s // tm, n // tn, k // tk),
        in_specs=[
            pl.BlockSpec((tm, tk), lambda i, j, kk: (i, kk)),
            pl.BlockSpec((tk, tn), lambda i, j, kk: (kk, j)),
            pl.BlockSpec((tm, tn), lambda i, j, kk: (i, j)),
            pl.BlockSpec((1, tn), lambda i, j, kk: (0, gate_idx * (n // tn) + j)),
        ],
        out_specs=pl.BlockSpec((tm, tn), lambda i, j, kk: (i, j)),
        out_shape=jax.ShapeDtypeStruct((s, n), F32),
        compiler_params=_cparams("arbitrary", "arbitrary", "arbitrary"),
        name="ff2",
    )(a, w, x, mod)


def _layer(x, c_col, pos_col, w_ada, b_ada, norm1_g, norm2_g, w_in, q_norm_g, k_norm_g, attn_sinks,
           lam_re, lam_im, log_dt, b_re, b_im, c_re, c_im, ssm_d, w_glu, w_out, w_ff1, w_ff2):
    s, d = x.shape
    kv_width = N_KV_HEADS * HEAD_DIM
    ssm_width = ssm_d.shape[0]
    n_groups = ssm_width // SSM_GROUP
    nc = s // CHUNK
    n_levels = max(1, (nc - 1).bit_length())

    b_row = b_ada.reshape(1, -1)
    mod_a, c_bcast = _modulation(c_col, w_ada, b_row, 2 * d)
    h, h_phase = _norm_mod(x, norm1_g.reshape(1, d), mod_a, 0, with_phase_major=True)
    rope_c, rope_sa, rope_sb = _rope_tables(pos_col)
    pz, mod_b = _proj(h, w_in, q_norm_g.reshape(1, -1), k_norm_g.reshape(1, -1), rope_c, rope_sa, rope_sb,
                      c_bcast, w_ada, b_row, 2 * d, d_model=d, kv_width=kv_width, ssm_width=ssm_width)
    u_col0 = d + 2 * kv_width
    u_t, gs_phase, w_glu_bf, w_out_bf = _proj_u(h_phase, w_in, w_glu, w_out, u_col0=u_col0,
                                                gs_col0=u_col0 + ssm_width + d, ssm_width=ssm_width, d_model=d)

    s5_ops = _s5_prep(log_dt, lam_re, lam_im, b_re, b_im, c_re, c_im, n_levels=n_levels)
    d_col = jnp.tile(ssm_d.reshape(n_groups, 1, SSM_GROUP), (1, CHUNK, 1)).reshape(n_groups, CHUNK * SSM_GROUP, 1)
    y_t = _s5_scan(u_t, s5_ops, d_col, n_levels=n_levels)

    attn_phase = _attention(attn_sinks, pz, d_model=d, kv_width=kv_width)
    merged_phase = _glu_merge(y_t, w_glu_bf, attn_phase, gs_phase)
    x1 = _out_proj(merged_phase, w_out_bf, x, mod_b, 0)

    (h2,) = _norm_mod(x1, norm2_g.reshape(1, d), mod_b, 1)
    act = _ff1(h2, w_ff1)
    return _ff2(act, w_ff2, x1, mod_b, 3)


def kernel(x, c, positions, w_ada, b_ada, norm1_g, norm2_g, w_in, q_norm_g, k_norm_g, attn_sinks, ssm_lam_re,
           ssm_lam_im, ssm_log_dt, ssm_b_re, ssm_b_im, ssm_c_re, ssm_c_im, ssm_d, w_glu, w_out, w_ff1, w_ff2):
    bsz, s, d = x.shape
    assert bsz == 1 and d == N_Q_HEADS * HEAD_DIM and s % (CHUNK * LANES) == 0
    xs = x[0]
    c_col = c.reshape(d, 1)
    pos_col = positions.reshape(s, 1)
    for l in range(w_ada.shape[0]):
        xs = _layer(xs, c_col, pos_col, w_ada[l], b_ada[l], norm1_g[l], norm2_g[l], w_in[l], q_norm_g[l],
                    k_norm_g[l], attn_sinks[l], ssm_lam_re[l], ssm_lam_im[l], ssm_log_dt[l], ssm_b_re[l],
                    ssm_b_im[l], ssm_c_re[l], ssm_c_im[l], ssm_d[l], w_glu[l], w_out[l], w_ff1[l], w_ff2[l])
    return xs[None]
```

```python
import functools
import math

import jax
import jax.numpy as jnp
from jax import lax
from jax.experimental import pallas as pl
from jax.experimental.pallas import tpu as pltpu

F32 = jnp.float32
BF16 = jnp.bfloat16

LANES = 128
SUBLANES = 8
MXU_DIM = 256
VMEM_LIMIT_BYTES = 56 * 1024 * 1024

N_Q_HEADS = 32
N_KV_HEADS = 8
HEAD_DIM = 128
REP = N_Q_HEADS // N_KV_HEADS
WINDOW = 128
ROPE_DIM = HEAD_DIM // 4
ROPE_HALF = ROPE_DIM // 2
ROPE_THETA = 500000.0
SSM_GROUP = 16
SSM_STATE = 64
N_MOD = 6
EPS = 1e-6
MASK_VALUE = -1e30
CHUNK = MXU_DIM // SSM_GROUP


def _phase_perm():
    n = CHUNK * CHUNK
    r = lax.broadcasted_iota(jnp.int32, (n, n), 0)
    c = lax.broadcasted_iota(jnp.int32, (n, n), 1)
    return jnp.where((r % CHUNK) * CHUNK + r // CHUNK == c, 1.0, 0.0).astype(BF16)


def _to_phase_major(v, perm):
    return jnp.dot(perm, v, preferred_element_type=F32).astype(v.dtype).reshape(CHUNK, CHUNK, v.shape[-1])


def _lagged_tiles(n_tiles, n_inner):
    def cur(step):
        t = jnp.minimum(step, n_tiles - 1)
        return t // n_inner, t % n_inner

    def prev(step):
        t = jnp.maximum(step - 1, 0)
        return t // n_inner, t % n_inner

    return cur, prev


def _overlapped_step(step, run_matmul, epilogues, accs):
    cases = [(step == 0, None)] + [((step > 0) & cond, fn) for cond, fn in epilogues]
    for parity in (0, 1):
        for cond, fn in cases:
            @pl.when(cond & (step % 2 == parity))
            def _(dst=accs[parity], src=accs[1 - parity], fn=fn):
                if fn is not None:
                    fn(src)
                run_matmul(dst)


def _side_slabs(n_tiles, rows):
    n_side = 1 << (n_tiles.bit_length() - 1)
    assert rows % (n_side * 2 * SUBLANES) == 0
    return n_side, rows // n_side


def _cparams(*sem):
    return pltpu.CompilerParams(dimension_semantics=sem, vmem_limit_bytes=VMEM_LIMIT_BYTES)


def _mod_kernel(c_ref, w_ref, b_ref, o_ref, cb_ref, *, rk):
    d, tn = w_ref.shape

    def body(r, acc):
        rows = pl.ds(pl.multiple_of(r * rk, rk), rk)
        c = c_ref[rows, :]
        ca = c * jax.nn.sigmoid(c)
        cb_ref[rows, :] = jnp.broadcast_to(ca, (rk, LANES))
        prod = w_ref[rows, :] * ca
        return acc + jnp.sum(prod.reshape(rk // SUBLANES, SUBLANES, tn), axis=0)

    acc = lax.fori_loop(0, d // rk, body, jnp.zeros((SUBLANES, tn), F32))
    o_ref[...] = jnp.sum(acc, axis=0, keepdims=True) + b_ref[...]


def _modulation(c_col, w_ada, b_ada_row, n_cols, *, tn=512, rk=512):
    d = w_ada.shape[0]
    return pl.pallas_call(
        functools.partial(_mod_kernel, rk=rk),
        grid=(n_cols // tn,),
        in_specs=[
            pl.BlockSpec((d, 1), lambda j: (0, 0)),
            pl.BlockSpec((d, tn), lambda j: (0, j)),
            pl.BlockSpec((1, tn), lambda j: (0, j)),
        ],
        out_specs=[pl.BlockSpec((1, tn), lambda j: (0, j)), pl.BlockSpec((d, LANES), lambda j: (0, 0))],
        out_shape=[jax.ShapeDtypeStruct((1, n_cols), F32), jax.ShapeDtypeStruct((d, LANES), F32)],
        compiler_params=_cparams("arbitrary"),
        name="mod",
    )(c_col, w_ada, b_ada_row)


def _mod_side_task(block, bias_col0, cb_ref, wm_ref, bm_ref, om_ref):
    d, wc = wm_ref.shape
    cb = cb_ref[...]
    for q in range(wc // LANES):
        part = jnp.sum((wm_ref[:, q * LANES:(q + 1) * LANES] * cb).reshape(d // SUBLANES, SUBLANES, LANES), axis=0)
        col = block * wc + q * LANES
        bias = bm_ref[:, pl.ds(pl.multiple_of(bias_col0 + col, LANES), LANES)]
        om_ref[:, pl.ds(pl.multiple_of(col, LANES), LANES)] = jnp.sum(part, axis=0, keepdims=True) + bias


def _norm_kernel(x_ref, g_ref, sh_ref, sc_ref, o_ref, *phase_ref):
    def normed(x):
        ms = jnp.mean(x * x, axis=-1, keepdims=True)
        y = x * lax.rsqrt(ms + EPS) * g_ref[...]
        return (y * (1.0 + sc_ref[...]) + sh_ref[...]).astype(o_ref.dtype)

    hb = normed(x_ref[...])
    o_ref[...] = hb
    for oph_ref in phase_ref:
        perm = _phase_perm()
        grp = CHUNK * CHUNK
        for q in range(hb.shape[0] // grp):
            oph_ref[:, q * CHUNK:(q + 1) * CHUNK, :] = _to_phase_major(hb[q * grp:(q + 1) * grp], perm)


def _norm_mod(x, g_row, mod, shift_idx, *, with_phase_major=False, tm=512):
    s, d = x.shape
    out_specs = [pl.BlockSpec((tm, d), lambda i: (i, 0))]
    out_shape = [jax.ShapeDtypeStruct((s, d), BF16)]
    if with_phase_major:
        out_specs.append(pl.BlockSpec((CHUNK, tm // CHUNK, d), lambda i: (0, i, 0)))
        out_shape.append(jax.ShapeDtypeStruct((CHUNK, s // CHUNK, d), BF16))
    return pl.pallas_call(
        _norm_kernel,
        grid=(s // tm,),
        in_specs=[
            pl.BlockSpec((tm, d), lambda i: (i, 0)),
            pl.BlockSpec((1, d), lambda i: (0, 0)),
            pl.BlockSpec((1, d), lambda i: (0, shift_idx)),
            pl.BlockSpec((1, d), lambda i: (0, shift_idx + 1)),
        ],
        out_specs=out_specs,
        out_shape=out_shape,
        compiler_params=_cparams("arbitrary"),
        name="norm",
    )(x, g_row, mod, mod)


def _rope_kernel(pos_ref, c_ref, sa_ref, sb_ref):
    pos = pos_ref[...].astype(F32)
    lane = lax.broadcasted_iota(jnp.int32, (1, LANES), 1)
    idx = (lane & (ROPE_HALF - 1)).astype(F32)
    inv_freq = jnp.power(jnp.float32(ROPE_THETA), -2.0 * idx / ROPE_DIM)
    ang = pos * inv_freq
    cos = jnp.cos(ang)
    sin = jnp.sin(ang)
    c_ref[...] = jnp.where(lane < ROPE_DIM, cos, 1.0)
    sa_ref[...] = jnp.where(lane < ROPE_HALF, -sin, 0.0)
    sb_ref[...] = jnp.where((lane >= ROPE_HALF) & (lane < ROPE_DIM), sin, 0.0)


def _rope_tables(pos_col, *, tm=1024):
    s = pos_col.shape[0]
    tm = min(tm, s)
    spec = pl.BlockSpec((tm, LANES), lambda i: (i, 0))
    shp = jax.ShapeDtypeStruct((s, LANES), F32)
    return pl.pallas_call(
        _rope_kernel,
        grid=(s // tm,),
        in_specs=[pl.BlockSpec((tm, 1), lambda i: (i, 0))],
        out_specs=[spec, spec, spec],
        out_shape=[shp, shp, shp],
        compiler_params=_cparams("arbitrary"),
        name="rope",
    )(pos_col)


def _proj_kernel(h_ref, w_ref, qg_ref, kg_ref, rc_ref, rsa_ref, rsb_ref, cb_ref, wm_ref, bm_ref, o_ref, om_ref,
                 acc0_ref, acc1_ref, *, n_j, n_q, n_qk, n_qkv, n_side, mod_col0):
    step = pl.program_id(0)
    jp = (step - 1) % n_j
    tn = o_ref.shape[1]

    def qk_epilogue(src):
        g = jnp.where(jp < n_q, qg_ref[...], kg_ref[...])
        c, sa, sb = rc_ref[...], rsa_ref[...], rsb_ref[...]
        for hh in range(tn // HEAD_DIM):
            cols = slice(hh * HEAD_DIM, (hh + 1) * HEAD_DIM)
            xh = src[:, cols]
            ms = jnp.mean(xh * xh, axis=-1, keepdims=True)
            y = xh * lax.rsqrt(ms + EPS) * g
            y = y * c + pltpu.roll(y, HEAD_DIM - ROPE_HALF, 1) * sa + pltpu.roll(y, ROPE_HALF, 1) * sb
            o_ref[:, cols] = y.astype(o_ref.dtype)

    def v_epilogue(src):
        o_ref[...] = src[...].astype(o_ref.dtype)

    def gate_epilogue(src):
        o_ref[...] = jax.nn.sigmoid(src[...]).astype(o_ref.dtype)

    def run_matmul(dst):
        _mod_side_task(jnp.minimum(step, n_side - 1), mod_col0, cb_ref, wm_ref, bm_ref, om_ref)
        dst[...] = jnp.dot(h_ref[...], w_ref[...].astype(BF16), preferred_element_type=F32)

    _overlapped_step(step, run_matmul,
                     [(jp < n_qk, qk_epilogue), ((jp >= n_qk) & (jp < n_qkv), v_epilogue), (jp >= n_qkv, gate_epilogue)],
                     (acc0_ref, acc1_ref))


def _proj(h, w_in, qg_row, kg_row, rope_c, rope_sa, rope_sb, c_bcast, w_ada, b_ada_row, mod_col0, *,
          d_model, kv_width, ssm_width, tm=1024, tn=512):
    s, d = h.shape
    tm = min(tm, s)
    n_q = d_model // tn
    n_qk = n_q + kv_width // tn
    n_qkv = n_qk + kv_width // tn
    n_u = ssm_width // tn
    n_j = n_qkv + d_model // tn
    n_tiles = (s // tm) * n_j
    cur, prev = _lagged_tiles(n_tiles, n_j)

    def w_map(t):
        j = cur(t)[1]
        return 0, jnp.where(j < n_qkv, j, j + n_u)

    n_mod = w_ada.shape[1] - mod_col0
    wc = LANES * pl.cdiv(n_mod // LANES, n_tiles)
    n_side = n_mod // wc
    assert n_mod % wc == 0 and mod_col0 % wc == 0

    def side(t):
        return jnp.minimum(t, n_side - 1)

    tab = pl.BlockSpec((tm, LANES), lambda t: (prev(t)[0], 0))
    row = pl.BlockSpec((1, HEAD_DIM), lambda t: (0, 0))
    return pl.pallas_call(
        functools.partial(_proj_kernel, n_j=n_j, n_q=n_q, n_qk=n_qk, n_qkv=n_qkv, n_side=n_side,
                          mod_col0=mod_col0),
        grid=(n_tiles + 1,),
        in_specs=[
            pl.BlockSpec((tm, d), lambda t: (cur(t)[0], 0)),
            pl.BlockSpec((d, tn), w_map),
            row, row, tab, tab, tab,
            pl.BlockSpec((d, LANES), lambda t: (0, 0)),
            pl.BlockSpec((d, wc), lambda t: (0, mod_col0 // wc + side(t))),
            pl.BlockSpec((1, w_ada.shape[1]), lambda t: (0, 0)),
        ],
        out_specs=[pl.BlockSpec((tm, tn), lambda t: prev(t)), pl.BlockSpec((1, n_mod), lambda t: (0, 0))],
        out_shape=[jax.ShapeDtypeStruct((s, n_j * tn), BF16), jax.ShapeDtypeStruct((1, n_mod), F32)],
        scratch_shapes=[pltpu.VMEM((tm, tn), F32), pltpu.VMEM((tm, tn), F32)],
        compiler_params=_cparams("arbitrary"),
        name="proj",
    )(h, w_in, qg_row, kg_row, rope_c, rope_sa, rope_sb, c_bcast, w_ada, b_ada_row)


def _proju_kernel(h_ref, w_ref, wglu_ref, wout_ref, ut_ref, gs_ref, wglu_bf_ref, wout_bf_ref, acc0_ref, acc1_ref, *,
                  n_j, n_u):
    step = pl.program_id(0)
    jp = (step - 1) % n_j
    n_ph, nc, d = h_ref.shape

    def u_epilogue(src):
        for p in range(n_ph):
            ut_ref[p] = src[p * nc:(p + 1) * nc, :].T

    def gate_epilogue(src):
        for p in range(n_ph):
            gs_ref[p] = jax.nn.sigmoid(src[p * nc:(p + 1) * nc, :]).astype(gs_ref.dtype)

    def run_matmul(dst):
        wglu_bf_ref[...] = wglu_ref[...].astype(wglu_bf_ref.dtype)
        wout_bf_ref[...] = wout_ref[...].astype(wout_bf_ref.dtype)
        dst[...] = jnp.dot(h_ref[...].reshape(n_ph * nc, d), w_ref[...].astype(BF16), preferred_element_type=F32)

    _overlapped_step(step, run_matmul, [(jp < n_u, u_epilogue), (jp >= n_u, gate_epilogue)], (acc0_ref, acc1_ref))


def _proj_u(h_phase, w_in, w_glu, w_out, *, u_col0, gs_col0, ssm_width, d_model, phases=2, tn=512):
    _, nc, d = h_phase.shape
    n_u = ssm_width // tn
    n_j = n_u + d_model // tn
    n_tiles = (CHUNK // phases) * n_j
    cur, prev = _lagged_tiles(n_tiles, n_j)
    n_cast, glu_rows = _side_slabs(n_tiles, w_glu.shape[0])
    _, out_rows = _side_slabs(n_tiles, w_out.shape[0])

    def cast_map(s):
        return jnp.minimum(s, n_cast - 1), 0

    def w_map(s):
        j = cur(s)[1]
        return 0, jnp.where(j < n_u, u_col0 // tn + j, gs_col0 // tn + j - n_u)

    def ut_map(s):
        t, j = prev(s)
        return t, jnp.minimum(j, n_u - 1), 0

    def gs_map(s):
        t, j = prev(s)
        return t, 0, jnp.maximum(j - n_u, 0)

    return pl.pallas_call(
        functools.partial(_proju_kernel, n_j=n_j, n_u=n_u),
        grid=(n_tiles + 1,),
        in_specs=[
            pl.BlockSpec((phases, nc, d), lambda s: (cur(s)[0], 0, 0)),
            pl.BlockSpec((d, tn), w_map),
            pl.BlockSpec((glu_rows, w_glu.shape[1]), cast_map),
            pl.BlockSpec((out_rows, w_out.shape[1]), cast_map),
        ],
        out_specs=[
            pl.BlockSpec((phases, tn, nc), ut_map),
            pl.BlockSpec((phases, nc, tn), gs_map),
            pl.BlockSpec((glu_rows, w_glu.shape[1]), cast_map),
            pl.BlockSpec((out_rows, w_out.shape[1]), cast_map),
        ],
        out_shape=[jax.ShapeDtypeStruct((CHUNK, ssm_width, nc), F32),
                   jax.ShapeDtypeStruct((CHUNK, nc, d_model), BF16),
                   jax.ShapeDtypeStruct(w_glu.shape, BF16),
                   jax.ShapeDtypeStruct(w_out.shape, BF16)],
        scratch_shapes=[pltpu.VMEM((phases * nc, tn), F32), pltpu.VMEM((phases * nc, tn), F32)],
        compiler_params=_cparams("arbitrary"),
        name="proj_u",
    )(h_phase, w_in, w_glu, w_out)


def _cmul(ar, ai, br, bi):
    return ar * br - ai * bi, ar * bi + ai * br


def _s5prep_kernel(ldt_ref, lr_ref, li_ref, bre_ref, bim_ref, cre_ref, cim_ref,
                   m_ref, wre_ref, wim_ref, vre_ref, vim_ref, apre_ref, apim_ref, *, n_levels):
    for gi in range(m_ref.shape[0]):
        _s5prep_group(gi, ldt_ref, lr_ref, li_ref, bre_ref, bim_ref, cre_ref, cim_ref,
                      m_ref, wre_ref, wim_ref, vre_ref, vim_ref, apre_ref, apim_ref, n_levels)


def _s5prep_group(gi, ldt_ref, lr_ref, li_ref, bre_ref, bim_ref, cre_ref, cim_ref,
                  m_ref, wre_ref, wim_ref, vre_ref, vim_ref, apre_ref, apim_ref, n_levels):
    p_ = SSM_STATE
    kdim = CHUNK * SSM_GROUP
    first = lax.broadcasted_iota(jnp.int32, (1, 2 * p_), 1) < p_
    own = first if gi % 2 == 0 else jnp.logical_not(first)
    dt = jnp.exp(ldt_ref[gi])
    lr, li = lr_ref[gi], li_ref[gi]
    mag = jnp.exp(lr * dt)
    are, aim = mag * jnp.cos(li * dt), mag * jnp.sin(li * dt)
    den = lr * lr + li * li
    nr = are - 1.0
    coef_re = (nr * lr + aim * li) / den
    coef_im = (aim * lr - nr * li) / den
    bt_re, bt_im = bre_ref[gi], bim_ref[gi]
    bbar_re = coef_re * bt_re - coef_im * bt_im
    bbar_im = coef_re * bt_im + coef_im * bt_re

    pow_re = [jnp.ones_like(are)]
    pow_im = [jnp.zeros_like(are)]
    for _ in range(CHUNK):
        nre, nim = _cmul(pow_re[-1], pow_im[-1], are, aim)
        pow_re.append(nre)
        pow_im.append(nim)

    blocks = []
    for m in range(CHUNK):
        g_re, g_im = _cmul(pow_re[CHUNK - 1 - m], pow_im[CHUNK - 1 - m], bbar_re, bbar_im)
        rows = slice(m * SSM_GROUP, (m + 1) * SSM_GROUP)
        wre_ref[gi, rows, :] = jnp.where(own, g_re, 0.0).astype(wre_ref.dtype)
        wim_ref[gi, rows, :] = jnp.where(own, g_im, 0.0).astype(wim_ref.dtype)
        blocks.append(jnp.where(first, g_re, g_im))
    g_t = jnp.concatenate(blocks, axis=0)

    cre2, cim2 = cre_ref[gi], cim_ref[gi]
    c_mix = jnp.where(first, cre2, -cim2)
    k_rev = lax.dot_general(c_mix, g_t, (((1,), (1,)), ((), ())), precision=lax.Precision.HIGHEST,
                            preferred_element_type=F32)
    k_ext = jnp.concatenate([k_rev, jnp.zeros_like(k_rev)], axis=1)
    for t in range(CHUNK):
        off = (CHUNK - 1 - t) * SSM_GROUP
        m_ref[gi, t * SSM_GROUP:(t + 1) * SSM_GROUP, :] = k_ext[:, off:off + kdim].astype(m_ref.dtype)

    for t in range(CHUNK):
        pr, pi = pow_re[t + 1], pow_im[t + 1]
        rows = slice(t * SSM_GROUP, (t + 1) * SSM_GROUP)
        vre_ref[gi, rows, :] = jnp.where(own, cre2 * pr - cim2 * pi, 0.0).astype(vre_ref.dtype)
        vim_ref[gi, rows, :] = jnp.where(own, -(cre2 * pi + cim2 * pr), 0.0).astype(vim_ref.dtype)

    sub = lax.broadcasted_iota(jnp.int32, (apre_ref.shape[1], 1), 0)
    qr, qi = pow_re[CHUNK], pow_im[CHUNK]
    lev_re = jnp.zeros(apre_ref.shape[1:], F32)
    lev_im = jnp.zeros(apre_ref.shape[1:], F32)
    for k in range(n_levels):
        lev_re = jnp.where((sub == k) & own, qr, lev_re)
        lev_im = jnp.where((sub == k) & own, qi, lev_im)
        qr, qi = _cmul(qr, qi, qr, qi)
    apre_ref[gi] = lev_re
    apim_ref[gi] = lev_im


def _s5_prep(log_dt, lam_re, lam_im, b_re, b_im, c_re, c_im, *, n_levels, gb=16):
    g, p_ = lam_re.shape
    assert gb % 2 == 0 and n_levels <= 2 * SUBLANES

    def dup(a):
        return jnp.concatenate([a, a], axis=-1)

    ldt = log_dt.reshape(g, 1, 1)
    lr, li = dup(lam_re).reshape(g, 1, 2 * p_), dup(lam_im).reshape(g, 1, 2 * p_)
    bt_re, bt_im = dup(jnp.swapaxes(b_re, 1, 2)), dup(jnp.swapaxes(b_im, 1, 2))
    kdim = CHUNK * SSM_GROUP

    def spec(*shape):
        return pl.BlockSpec((gb,) + shape, lambda i: (i, 0, 0))

    op = jax.ShapeDtypeStruct((g, kdim, 2 * p_), BF16)
    lev = jax.ShapeDtypeStruct((g, 2 * SUBLANES, 2 * p_), F32)
    return pl.pallas_call(
        functools.partial(_s5prep_kernel, n_levels=n_levels),
        grid=(g // gb,),
        in_specs=[spec(1, 1), spec(1, 2 * p_), spec(1, 2 * p_), spec(SSM_GROUP, 2 * p_), spec(SSM_GROUP, 2 * p_),
                  spec(SSM_GROUP, 2 * p_), spec(SSM_GROUP, 2 * p_)],
        out_specs=[spec(kdim, kdim)] + [spec(kdim, 2 * p_)] * 4 + [spec(2 * SUBLANES, 2 * p_)] * 2,
        out_shape=[jax.ShapeDtypeStruct((g, kdim, kdim), BF16), op, op, op, op, lev, lev],
        compiler_params=_cparams("arbitrary"),
        name="s5prep",
    )(ldt, lr, li, bt_re, bt_im, dup(c_re), dup(c_im))


def _s5_kernel(x_ref, m_ref, wre_ref, wim_ref, vre_ref, vim_ref, apre_ref, apim_ref, d_ref, o_ref, *, n_levels):
    for pair in range(m_ref.shape[0] // 2):
        _s5_pair(pair, x_ref, m_ref, wre_ref, wim_ref, vre_ref, vim_ref, apre_ref, apim_ref, d_ref, o_ref, n_levels)


def _s5_pair(pair, x_ref, m_ref, wre_ref, wim_ref, vre_ref, vim_ref, apre_ref, apim_ref, d_ref, o_ref, n_levels):
    kdim = CHUNK * SSM_GROUP
    nc = x_ref.shape[-1]
    groups = (2 * pair, 2 * pair + 1)
    chans = [slice(g * SSM_GROUP, (g + 1) * SSM_GROUP) for g in groups]
    xs = [x_ref[:, c, :].reshape(kdim, nc) for c in chans]
    xbs = [x.astype(BF16) for x in xs]
    xcat = jnp.concatenate(xbs, axis=0)
    tn_ = (((0,), (0,)), ((), ()))
    nt = (((1,), (1,)), ((), ()))

    def pair_rows(ref):
        return jnp.concatenate([ref[g] for g in groups], axis=0)

    s_re = lax.dot_general(xcat, pair_rows(wre_ref), tn_, preferred_element_type=F32)
    s_im = lax.dot_general(xcat, pair_rows(wim_ref), tn_, preferred_element_type=F32)
    row = lax.broadcasted_iota(jnp.int32, (nc, 1), 0)

    def shift_rows(v, sh):
        if sh % SUBLANES == 0:
            return jnp.concatenate([jnp.zeros((sh, v.shape[1]), v.dtype), v[:nc - sh]], axis=0)
        return jnp.where(row >= sh, pltpu.roll(v, sh, 0), 0.0)

    lev_re = apre_ref[groups[0]] + apre_ref[groups[1]]
    lev_im = apim_ref[groups[0]] + apim_ref[groups[1]]
    for k in range(n_levels):
        sh = 1 << k
        pr, pi = lev_re[k:k + 1], lev_im[k:k + 1]
        t_re, t_im = shift_rows(s_re, sh), shift_rows(s_im, sh)
        s_re, s_im = s_re + (pr * t_re - pi * t_im), s_im + (pr * t_im + pi * t_re)
    st_re = shift_rows(s_re, 1).astype(BF16)
    st_im = shift_rows(s_im, 1).astype(BF16)

    for g, c, x, xb in zip(groups, chans, xs, xbs):
        y = jnp.dot(m_ref[g], xb, preferred_element_type=F32)
        y = y + lax.dot_general(vre_ref[g], st_re, nt, preferred_element_type=F32)
        y = y + lax.dot_general(vim_ref[g], st_im, nt, preferred_element_type=F32)
        y = y + d_ref[g] * x
        o_ref[:, c, :] = jax.nn.gelu(y).astype(o_ref.dtype).reshape(CHUNK, SSM_GROUP, nc)


def _s5_scan(u_t, ops, d_col, *, n_levels, gb=8):
    _, width, nc = u_t.shape
    g = width // SSM_GROUP
    kdim = CHUNK * SSM_GROUP
    p2 = 2 * SSM_STATE

    def spec(*shape):
        return pl.BlockSpec((gb,) + shape, lambda i: (i, 0, 0))

    io = pl.BlockSpec((CHUNK, gb * SSM_GROUP, nc), lambda i: (0, i, 0))
    return pl.pallas_call(
        functools.partial(_s5_kernel, n_levels=n_levels),
        grid=(g // gb,),
        in_specs=[io, spec(kdim, kdim)] + [spec(kdim, p2)] * 4 + [spec(2 * SUBLANES, p2)] * 2 + [spec(kdim, 1)],
        out_specs=io,
        out_shape=jax.ShapeDtypeStruct((CHUNK, width, nc), BF16),
        compiler_params=_cparams("arbitrary"),
        name="s5",
    )(u_t, *ops, d_col)


def _attn_kernel(sink_ref, q_ref, kc_ref, kp_ref, vc_ref, vp_ref, ga_ref, o_ref, nat_ref, *, n_groups):
    i = pl.program_id(0)
    gh = pl.program_id(1)
    tq = q_ref.shape[0]
    nb = tq // WINDOW
    scale = HEAD_DIM ** -0.5
    row = lax.broadcasted_iota(jnp.int32, (WINDOW, WINDOW), 0)
    col = lax.broadcasted_iota(jnp.int32, (WINDOW, WINDOW), 1)
    tri1 = col <= row
    tri = jnp.concatenate([tri1] * REP, axis=0)
    nt = (((1,), (1,)), ((), ()))
    for b in range(nb):
        rows = slice(b * WINDOW, (b + 1) * WINDOW)
        has_prev = (i * nb + b) > 0
        valid = tri | has_prev
        for g in range(n_groups):
            kcols = slice(g * HEAD_DIM, (g + 1) * HEAD_DIM)
            if b == 0:
                k_prev, v_prev = kp_ref[:, kcols], vp_ref[:, kcols]
            else:
                prows = slice((b - 1) * WINDOW, b * WINDOW)
                k_prev, v_prev = kc_ref[prows, kcols], vc_ref[prows, kcols]
            k_band = jnp.concatenate([k_prev, kc_ref[rows, kcols]], axis=0)
            v_band = jnp.concatenate([v_prev, vc_ref[rows, kcols]], axis=0)
            qcols = [slice((g * REP + r) * HEAD_DIM, (g * REP + r + 1) * HEAD_DIM) for r in range(REP)]
            qg = jnp.concatenate([q_ref[rows, c] for c in qcols], axis=0)
            s2 = lax.dot_general(qg, k_band, nt, preferred_element_type=F32)
            s = jnp.where(tri, s2[:, WINDOW:], s2[:, :WINDOW]) * scale
            s = jnp.where(valid, s, MASK_VALUE)
            probs = []
            for r in range(REP):
                sink = sink_ref[(gh * n_groups + g) * REP + r]
                sr = s[r * WINDOW:(r + 1) * WINDOW]
                m = jnp.maximum(jnp.max(sr, axis=-1, keepdims=True), sink)
                p = jnp.exp(sr - m)
                denom = jnp.sum(p, axis=-1, keepdims=True) + jnp.exp(sink - m)
                probs.append(p * (1.0 / denom))
            pr = jnp.concatenate(probs, axis=0)
            p2 = jnp.concatenate([jnp.where(tri, 0.0, pr), jnp.where(tri, pr, 0.0)], axis=1).astype(BF16)
            o = jnp.dot(p2, v_band, preferred_element_type=F32)
            for r in range(REP):
                gate = ga_ref[rows, qcols[r]].astype(F32)
                nat_ref[rows, qcols[r]] = gate * o[r * WINDOW:(r + 1) * WINDOW]
    perm = _phase_perm()
    grp = CHUNK * CHUNK
    for q in range(tq // grp):
        nat = nat_ref[q * grp:(q + 1) * grp, :].astype(o_ref.dtype)
        o_ref[:, q * CHUNK:(q + 1) * CHUNK, :] = _to_phase_major(nat, perm)


def _attention(sinks, pz, *, d_model, kv_width, tq=1024, n_groups=4):
    s = pz.shape[0]
    tq = min(tq, s)
    nb = tq // WINDOW
    wq = n_groups * REP * HEAD_DIM
    wk = n_groups * HEAD_DIM
    k0 = d_model // wk
    v0 = (d_model + kv_width) // wk
    a0 = (d_model + 2 * kv_width) // wq

    def prev_map(col0):
        return lambda i, gh: (jnp.maximum(i * nb - 1, 0), col0 + gh)

    return pl.pallas_call(
        functools.partial(_attn_kernel, n_groups=n_groups),
        grid=(s // tq, d_model // wq),
        in_specs=[
            pl.BlockSpec(memory_space=pltpu.SMEM),
            pl.BlockSpec((tq, wq), lambda i, gh: (i, gh)),
            pl.BlockSpec((tq, wk), lambda i, gh: (i, k0 + gh)),
            pl.BlockSpec((WINDOW, wk), prev_map(k0)),
            pl.BlockSpec((tq, wk), lambda i, gh: (i, v0 + gh)),
            pl.BlockSpec((WINDOW, wk), prev_map(v0)),
            pl.BlockSpec((tq, wq), lambda i, gh: (i, a0 + gh)),
        ],
        out_specs=pl.BlockSpec((CHUNK, tq // CHUNK, wq), lambda i, gh: (0, i, gh)),
        out_shape=jax.ShapeDtypeStruct((CHUNK, s // CHUNK, d_model), BF16),
        scratch_shapes=[pltpu.VMEM((tq, wq), F32)],
        compiler_params=_cparams("arbitrary", "arbitrary"),
        name="attn",
    )(sinks, pz, pz, pz, pz, pz, pz)


def _glu_kernel(y_ref, wa_ref, wb_ref, at_ref, gs_ref, o_ref, yt_ref):
    n_ph, _, nc = y_ref.shape

    @pl.when(pl.program_id(1) == 0)
    def _():
        for p in range(n_ph):
            yt_ref[p * nc:(p + 1) * nc, :] = y_ref[p].T

    y = yt_ref[...]
    a = jnp.dot(y, wa_ref[...], preferred_element_type=F32)
    b = jnp.dot(y, wb_ref[...], preferred_element_type=F32)
    branch = a * jax.nn.sigmoid(b)
    for p in range(n_ph):
        rows = slice(p * nc, (p + 1) * nc)
        o_ref[p] = (at_ref[p].astype(F32) + gs_ref[p].astype(F32) * branch[rows]).astype(o_ref.dtype)


def _glu_merge(y_t, w_glu, attn_phase, gs_phase, *, phases=2, tn=1024):
    _, width, nc = y_t.shape
    d_model = attn_phase.shape[-1]
    nj = d_model // tn
    blk = pl.BlockSpec((phases, nc, tn), lambda t, j: (t, 0, j))
    return pl.pallas_call(
        _glu_kernel,
        grid=(CHUNK // phases, nj),
        in_specs=[
            pl.BlockSpec((phases, width, nc), lambda t, j: (t, 0, 0)),
            pl.BlockSpec((width, tn), lambda t, j: (0, j)),
            pl.BlockSpec((width, tn), lambda t, j: (0, nj + j)),
            blk, blk,
        ],
        out_specs=blk,
        out_shape=jax.ShapeDtypeStruct(attn_phase.shape, BF16),
        scratch_shapes=[pltpu.VMEM((phases * nc, width), BF16)],
        compiler_params=_cparams("arbitrary", "arbitrary"),
        name="glu",
    )(y_t, w_glu, w_glu, attn_phase, gs_phase)


def _outp_kernel(a_ref, w_ref, x_ref, g_ref, o_ref, nat_ref):
    grp = CHUNK * CHUNK

    @pl.when(pl.program_id(1) == 0)
    def _():
        for q in range(nat_ref.shape[0] // grp):
            a = jnp.swapaxes(a_ref[:, q * CHUNK:(q + 1) * CHUNK, :], 0, 1)
            nat_ref[q * grp:(q + 1) * grp, :] = a.reshape(grp, a_ref.shape[-1])

    acc = jnp.dot(nat_ref[...], w_ref[...], preferred_element_type=F32)
    o_ref[...] = x_ref[...] + g_ref[...] * acc


def _out_proj(a_phase, w, x, mod, gate_idx, *, tm=1024, tn=512):
    n_ph, nc, k = a_phase.shape
    s, n = x.shape
    tm = min(tm, s)
    return pl.pallas_call(
        _outp_kernel,
        grid=(s // tm, n // tn),
        in_specs=[
            pl.BlockSpec((n_ph, tm // n_ph, k), lambda i, j: (0, i, 0)),
            pl.BlockSpec((k, tn), lambda i, j: (0, j)),
            pl.BlockSpec((tm, tn), lambda i, j: (i, j)),
            pl.BlockSpec((1, tn), lambda i, j: (0, gate_idx * (n // tn) + j)),
        ],
        out_specs=pl.BlockSpec((tm, tn), lambda i, j: (i, j)),
        out_shape=jax.ShapeDtypeStruct((s, n), F32),
        scratch_shapes=[pltpu.VMEM((tm, k), BF16)],
        compiler_params=_cparams("arbitrary", "arbitrary"),
        name="outp",
    )(a_phase, w, x, mod)


def _ff1_kernel(h_ref, w_ref, o_ref):
    acc = jnp.dot(h_ref[...], w_ref[...].astype(BF16), preferred_element_type=F32)
    r = jnp.maximum(acc, 0.0)
    o_ref[...] = (r * r).astype(o_ref.dtype)


def _ff1(h, w, *, tm=1024, tn=512):
    s, k = h.shape
    n = w.shape[1]
    tm = min(tm, s)
    return pl.pallas_call(
        _ff1_kernel,
        grid=(s // tm, n // tn),
        in_specs=[pl.BlockSpec((tm, k), lambda i, j: (i, 0)), pl.BlockSpec((k, tn), lambda i, j: (0, j))],
        out_specs=pl.BlockSpec((tm, tn), lambda i, j: (i, j)),
        out_shape=jax.ShapeDtypeStruct((s, n), BF16),
        compiler_params=_cparams("arbitrary", "arbitrary"),
        name="ff1",
    )(h, w)


def _ff2_kernel(a_ref, w_ref, x_ref, g_ref, o_ref):
    kk = pl.program_id(2)
    last = pl.num_programs(2) - 1

    def product():
        return jnp.dot(a_ref[...], w_ref[...].astype(BF16), preferred_element_type=F32)

    @pl.when(kk == 0)
    def _():
        o_ref[...] = product()

    @pl.when((kk > 0) & (kk < last))
    def _():
        o_ref[...] += product()

    @pl.when(kk == last)
    def _():
        o_ref[...] = x_ref[...] + g_ref[...] * (o_ref[...] + product())


def _ff2(a, w, x, mod, gate_idx, *, tm=2048, tn=1024, tk=1024):
    s, k = a.shape
    n = w.shape[1]
    tm = min(tm, s)
    assert k // tk >= 2
    return pl.pallas_call(
        _ff2_kernel,
        grid=(s // tm, n // tn, k // tk),
        in_specs=[
            pl.BlockSpec((tm, tk), lambda i, j, kk: (i, kk)),
            pl.BlockSpec((tk, tn), lambda i, j, kk: (kk, j)),
            pl.BlockSpec((tm, tn), lambda i, j, kk: (i, j)),
            pl.BlockSpec((1, tn), lambda i, j, kk: (0, gate_idx * (n // tn) + j)),
        ],
        out_specs=pl.BlockSpec((tm, tn), lambda i, j, kk: (i, j)),
        out_shape=jax.ShapeDtypeStruct((s, n), F32),
        compiler_params=_cparams("arbitrary", "arbitrary", "arbitrary"),
        name="ff2",
    )(a, w, x, mod)


def _layer(x, c_col, pos_col, w_ada, b_ada, norm1_g, norm2_g, w_in, q_norm_g, k_norm_g, attn_sinks,
           lam_re, lam_im, log_dt, b_re, b_im, c_re, c_im, ssm_d, w_glu, w_out, w_ff1, w_ff2):
    s, d = x.shape
    kv_width = N_KV_HEADS * HEAD_DIM
    ssm_width = ssm_d.shape[0]
    n_groups = ssm_width // SSM_GROUP
    nc = s // CHUNK
    n_levels = max(1, (nc - 1).bit_length())

    b_row = b_ada.reshape(1, -1)
    mod_a, c_bcast = _modulation(c_col, w_ada, b_row, 2 * d)
    h, h_phase = _norm_mod(x, norm1_g.reshape(1, d), mod_a, 0, with_phase_major=True)
    rope_c, rope_sa, rope_sb = _rope_tables(pos_col)
    pz, mod_b = _proj(h, w_in, q_norm_g.reshape(1, -1), k_norm_g.reshape(1, -1), rope_c, rope_sa, rope_sb,
                      c_bcast, w_ada, b_row, 2 * d, d_model=d, kv_width=kv_width, ssm_width=ssm_width)
    u_col0 = d + 2 * kv_width
    u_t, gs_phase, w_glu_bf, w_out_bf = _proj_u(h_phase, w_in, w_glu, w_out, u_col0=u_col0,
                                                gs_col0=u_col0 + ssm_width + d, ssm_width=ssm_width, d_model=d)

    s5_ops = _s5_prep(log_dt, lam_re, lam_im, b_re, b_im, c_re, c_im, n_levels=n_levels)
    d_col = jnp.tile(ssm_d.reshape(n_groups, 1, SSM_GROUP), (1, CHUNK, 1)).reshape(n_groups, CHUNK * SSM_GROUP, 1)
    y_t = _s5_scan(u_t, s5_ops, d_col, n_levels=n_levels)

    attn_phase = _attention(attn_sinks, pz, d_model=d, kv_width=kv_width)
    merged_phase = _glu_merge(y_t, w_glu_bf, attn_phase, gs_phase)
    x1 = _out_proj(merged_phase, w_out_bf, x, mod_b, 0)

    (h2,) = _norm_mod(x1, norm2_g.reshape(1, d), mod_b, 1)
    act = _ff1(h2, w_ff1)
    return _ff2(act, w_ff2, x1, mod_b, 3)


def kernel(x, c, positions, w_ada, b_ada, norm1_g, norm2_g, w_in, q_norm_g, k_norm_g, attn_sinks, ssm_lam_re,
           ssm_lam_im, ssm_log_dt, ssm_b_re, ssm_b_im, ssm_c_re, ssm_c_im, ssm_d, w_glu, w_out, w_ff1, w_ff2):
    bsz, s, d = x.shape
    assert bsz == 1 and d == N_Q_HEADS * HEAD_DIM and s % (CHUNK * LANES) == 0
    xs = x[0]
    c_col = c.reshape(d, 1)
    pos_col = positions.reshape(s, 1)
    for l in range(w_ada.shape[0]):
        xs = _layer(xs, c_col, pos_col, w_ada[l], b_ada[l], norm1_g[l], norm2_g[l], w_in[l], q_norm_g[l],
                    k_norm_g[l], attn_sinks[l], ssm_lam_re[l], ssm_lam_im[l], ssm_log_dt[l], ssm_b_re[l],
                    ssm_b_im[l], ssm_c_re[l], ssm_c_im[l], ssm_d[l], w_glu[l], w_out[l], w_ff1[l], w_ff2[l])
    return xs[None]
```
